```python
import jax, jax.numpy as jnp
from jax import lax
import numpy as np

D_MODEL = 1024
BATCH = 1
SEQ = 16384
DEPTH = 4

N_MIXERS = 2
N_HEADS = 16
HEAD_DIM = D_MODEL // N_HEADS
MOBA_BLOCK = 256
MOBA_TOPK = 3
Q_CHUNK = 128
POOL_WINDOWS = (2, 4, 8, 16)
N_POOL_GROUPS = len(POOL_WINDOWS)
POOL_GROUP_DIM = D_MODEL // N_POOL_GROUPS
D_FF = ((8 * D_MODEL // 3 + 127) // 128) * 128
N_ATTN_LAYERS = (DEPTH + N_MIXERS - 1) // N_MIXERS
N_POOL_LAYERS = DEPTH // N_MIXERS
EPS = 1e-6
NEG = -1e30

kernel_name = "hybrid_moba_pool_macaron"


def rmsnorm(x, g):
    xf = x.astype(jnp.float32)
    y = xf * lax.rsqrt(jnp.mean(xf * xf, axis=-1, keepdims=True) + EPS)
    return (y * g.astype(jnp.float32)).astype(x.dtype)


def swiglu(h, w_gate, w_up, w_down):
    return (jax.nn.silu(h @ w_gate) * (h @ w_up)) @ w_down


def alibi_slopes(n_heads):
    return jnp.exp2(-8.0 * jnp.arange(1, n_heads + 1, dtype=jnp.float32) / n_heads)


def moba_attention(h, w_qkv, w_o):
    B, S, D = h.shape
    H, dh, BS = N_HEADS, HEAD_DIM, MOBA_BLOCK
    qkv = h @ w_qkv
    q, k, v = jnp.split(qkv, 3, axis=-1)
    q = q.reshape(B, S, H, dh).astype(jnp.float32) * (dh ** -0.5)
    k = k.reshape(B, S, H, dh).astype(jnp.float32)
    v = v.reshape(B, S, H, dh).astype(jnp.float32)
    nb = -(-S // BS)
    pad = nb * BS - S
    k_pad = jnp.pad(k, ((0, 0), (0, pad), (0, 0), (0, 0)))
    v_pad = jnp.pad(v, ((0, 0), (0, pad), (0, 0), (0, 0)))
    k_blk = k_pad.reshape(B, nb, BS, H, dh).transpose(0, 3, 1, 2, 4)
    v_blk = v_pad.reshape(B, nb, BS, H, dh).transpose(0, 3, 1, 2, 4)
    k_mean = jnp.mean(k_blk, axis=3)
    slopes = alibi_slopes(H)
    n_sel = min(MOBA_TOPK, nb)
    b_ix = jnp.arange(B)[:, None, None, None]
    h_ix = jnp.arange(H)[None, None, :, None]

    def chunk(ci):
        t0 = ci * Q_CHUNK
        qc = lax.dynamic_slice_in_dim(q, t0, Q_CHUNK, axis=1)
        tpos = t0 + jnp.arange(Q_CHUNK, dtype=jnp.int32)
        j = t0 // BS
        gate = jnp.einsum('bchd,bhnd->bchn', qc, k_mean)
        gate = jnp.where(jnp.arange(nb) < j, gate, NEG)
        _, sel = lax.top_k(gate, n_sel)
        valid = jnp.arange(n_sel) < j
        kg = k_blk[b_ix, h_ix, sel]
        vg = v_blk[b_ix, h_ix, sel]
        kpos_sel = sel[..., None] * BS + jnp.arange(BS, dtype=jnp.int32)
        dist_sel = (tpos[None, :, None, None, None] - kpos_sel).astype(jnp.float32)
        s_sel = jnp.einsum('bchd,bchksd->bchks', qc, kg)
        s_sel = s_sel - slopes[None, None, :, None, None] * dist_sel
        s_sel = jnp.where(valid[None, None, None, :, None], s_sel, NEG)
        k_own = lax.dynamic_slice_in_dim(k_pad, j * BS, BS, axis=1)
        v_own = lax.dynamic_slice_in_dim(v_pad, j * BS, BS, axis=1)
        kpos_own = j * BS + jnp.arange(BS, dtype=jnp.int32)
        dist_own = tpos[:, None] - kpos_own[None, :]
        s_own = jnp.einsum('bchd,bshd->bchs', qc, k_own)
        s_own = s_own - slopes[None, None, :, None] * dist_own.astype(jnp.float32)[None, :, None, :]
        s_own = jnp.where((dist_own >= 0)[None, :, None, :], s_own, NEG)
        scores = jnp.concatenate([s_sel.reshape(B, Q_CHUNK, H, n_sel * BS), s_own], axis=-1)
        p = jax.nn.softmax(scores, axis=-1)
        p_sel = p[..., :n_sel * BS].reshape(B, Q_CHUNK, H, n_sel, BS)
        o = (jnp.einsum('bchks,bchksd->bchd', p_sel, vg)
             + jnp.einsum('bchs,bshd->bchd', p[..., n_sel * BS:], v_own))
        return o

    outs = lax.map(chunk, jnp.arange(S // Q_CHUNK, dtype=jnp.int32))
    o = outs.transpose(1, 0, 2, 3, 4).reshape(B, S, D).astype(h.dtype)
    return o @ w_o


def pool_mixer(h, w_pool, scale):
    B, S, D = h.shape
    G = POOL_GROUP_DIM
    hf = h.astype(jnp.float32)
    csum = jnp.concatenate([jnp.zeros((B, 1, D), jnp.float32), lax.cumsum(hf, axis=1)], axis=1)
    t = jnp.arange(S, dtype=jnp.int32)
    groups = []
    for g, w in enumerate(POOL_WINDOWS):
        c = csum[:, :, g * G:(g + 1) * G]
        lo = jnp.maximum(t + 1 - w, 0)
        win_sum = c[:, 1:] - c[:, lo]
        cnt = jnp.minimum(t + 1, w).astype(jnp.float32)
        groups.append(win_sum / cnt[None, :, None] - hf[:, :, g * G:(g + 1) * G])
    y = jnp.stack(groups, axis=2).astype(h.dtype)
    y = jnp.einsum('bsgc,gce->bsge', y, w_pool).reshape(B, S, D)
    return y * scale


def setup_inputs(seed: int = 0) -> dict:
    key = jax.random.key(seed)
    ks = jax.random.split(key, 11)
    f32 = jnp.float32
    x = jax.random.normal(ks[0], (BATCH, SEQ, D_MODEL), f32)
    ln_gains = 1.0 + 0.02 * jax.random.normal(ks[1], (DEPTH, 3, D_MODEL), f32)
    ffn_w_gate = jax.random.normal(ks[2], (DEPTH, 2, D_MODEL, D_FF), f32) * D_MODEL ** -0.5
    ffn_w_up = jax.random.normal(ks[3], (DEPTH, 2, D_MODEL, D_FF), f32) * D_MODEL ** -0.5
    ffn_w_down = jax.random.normal(ks[4], (DEPTH, 2, D_FF, D_MODEL), f32) * D_FF ** -0.5
    attn_w_qkv = jax.random.normal(ks[5], (N_ATTN_LAYERS, D_MODEL, 3 * D_MODEL), f32) * D_MODEL ** -0.5
    attn_w_o = jax.random.normal(ks[6], (N_ATTN_LAYERS, D_MODEL, D_MODEL), f32) * D_MODEL ** -0.5
    pool_w = jax.random.normal(ks[7], (N_POOL_LAYERS, N_POOL_GROUPS, POOL_GROUP_DIM, POOL_GROUP_DIM), f32) * POOL_GROUP_DIM ** -0.5
    pool_scale = 1.0 + 0.1 * jax.random.normal(ks[8], (N_POOL_LAYERS, D_MODEL), f32)
    final_gain = 1.0 + 0.02 * jax.random.normal(ks[9], (D_MODEL,), f32)
    return {"x": x, "ln_gains": ln_gains, "ffn_w_gate": ffn_w_gate, "ffn_w_up": ffn_w_up,
            "ffn_w_down": ffn_w_down, "attn_w_qkv": attn_w_qkv, "attn_w_o": attn_w_o,
            "pool_w": pool_w, "pool_scale": pool_scale, "final_gain": final_gain}


def reference(x, ln_gains, ffn_w_gate, ffn_w_up, ffn_w_down, attn_w_qkv, attn_w_o, pool_w, pool_scale, final_gain):
    for i in range(DEPTH):
        h = rmsnorm(x, ln_gains[i, 0])
        x = x + 0.5 * swiglu(h, ffn_w_gate[i, 0], ffn_w_up[i, 0], ffn_w_down[i, 0])
        h = rmsnorm(x, ln_gains[i, 1])
        m = i // N_MIXERS
        if i % N_MIXERS == 0:
            x = x + moba_attention(h, attn_w_qkv[m], attn_w_o[m])
        else:
            x = x + pool_mixer(h, pool_w[m], pool_scale[m])
        h = rmsnorm(x, ln_gains[i, 2])
        x = x + 0.5 * swiglu(h, ffn_w_gate[i, 1], ffn_w_up[i, 1], ffn_w_down[i, 1])
    return rmsnorm(x, final_gain)
```

```python
import functools

import jax
import jax.numpy as jnp
from jax import lax
from jax.experimental import pallas as pl
from jax.experimental.pallas import tpu as pltpu

N_HEADS = 16
HEAD_DIM = 64
MOBA_BLOCK = 256
MOBA_TOPK = 3
POOL_WINDOWS = (2, 4, 8, 16)
EPS = 1e-6
NEG = -1e30

LANES = 128
HEADS_PER_STEP = LANES // HEAD_DIM
POOL_HALO = 16
VMEM_LIMIT = 56 * 1024 * 1024

F32 = jnp.float32
BF16 = jnp.bfloat16


def _params():
    return pltpu.CompilerParams(
        dimension_semantics=("arbitrary",), vmem_limit_bytes=VMEM_LIMIT)


def _resident(shape):
    return pl.BlockSpec(shape, lambda *_: (0,) * len(shape),
                        pipeline_mode=pl.Buffered(1))


def _rmsnorm(x, g):
    ms = jnp.mean(x * x, axis=-1, keepdims=True)
    return x * lax.rsqrt(ms + EPS) * g


def _ffn_kernel(x_ref, g_ref, wg_ref, wu_ref, wd_ref, fg_ref, o_ref, a_ref, *,
                ff_chunk, final_norm):
    x = x_ref[...]
    h = _rmsnorm(x, g_ref[...]).astype(BF16)
    d_ff = wg_ref.shape[1]
    for c in range(d_ff // ff_chunk):
        sl = slice(c * ff_chunk, (c + 1) * ff_chunk)
        gate = jnp.dot(h, wg_ref[:, sl], preferred_element_type=F32)
        up = jnp.dot(h, wu_ref[:, sl], preferred_element_type=F32)
        a_ref[:, sl] = (gate * jax.nn.sigmoid(gate) * up).astype(BF16)
    y = x + 0.5 * jnp.dot(a_ref[...], wd_ref[...], preferred_element_type=F32)
    if final_norm:
        y = _rmsnorm(y, fg_ref[...])
    o_ref[...] = y


def _ffn(x, gain, wg, wu, wd, final_gain, *, final_norm, tm=512, ff_chunk=256):
    s, d = x.shape
    d_ff = wg.shape[1]
    row = pl.BlockSpec((tm, d), lambda i: (i, 0))
    return pl.pallas_call(
        functools.partial(_ffn_kernel, ff_chunk=ff_chunk, final_norm=final_norm),
        grid=(s // tm,),
        in_specs=[row, _resident((1, d)), _resident((d, d_ff)),
                  _resident((d, d_ff)), _resident((d_ff, d)), _resident((1, d))],
        out_specs=row,
        out_shape=jax.ShapeDtypeStruct((s, d), F32),
        scratch_shapes=[pltpu.VMEM((tm, d_ff), BF16)],
        compiler_params=_params(),
        name="ffn",
    )(x, gain, wg, wu, wd, final_gain)


def _qkv_kernel(x_ref, g_ref, w_ref, q_ref, k_ref, v_ref, km_ref):
    d = x_ref.shape[1]
    h = _rmsnorm(x_ref[...], g_ref[...]).astype(BF16)
    qkv = jnp.dot(h, w_ref[...], preferred_element_type=F32)
    q_ref[...] = (qkv[:, :d] * (HEAD_DIM ** -0.5)).astype(BF16)
    k = qkv[:, d:2 * d]
    k_ref[...] = k.astype(BF16)
    v_ref[...] = qkv[:, 2 * d:].astype(BF16)
    for b in range(x_ref.shape[0] // MOBA_BLOCK):
        kb = k[b * MOBA_BLOCK:(b + 1) * MOBA_BLOCK]
        km_ref[b] = jnp.mean(kb, axis=0, keepdims=True)


def _qkv(x, gain, w, *, tm=512):
    s, d = x.shape
    row = pl.BlockSpec((tm, d), lambda i: (i, 0))
    nb_tile = tm // MOBA_BLOCK
    return pl.pallas_call(
        _qkv_kernel,
        grid=(s // tm,),
        in_specs=[row, _resident((1, d)), _resident((d, 3 * d))],
        out_specs=[row, row, row,
                   pl.BlockSpec((nb_tile, 1, d), lambda i: (i, 0, 0))],
        out_shape=[jax.ShapeDtypeStruct((s, d), BF16)] * 3
        + [jax.ShapeDtypeStruct((s // MOBA_BLOCK, 1, d), F32)],
        compiler_params=_params(),
        name="qkv",
    )(x, gain, w)


def _attn_kernel(slopes_ref, q_ref, k_ref, v_ref, km_ref, o_ref):
    pair = pl.program_id(0)
    jq = pl.program_id(1)
    bs = MOBA_BLOCK
    nb = km_ref.shape[0]
    q = q_ref[...]
    lane = lax.broadcasted_iota(jnp.int32, (1, LANES), 1)
    blk = lax.broadcasted_iota(jnp.int32, (bs, nb), 1)
    r_minus_c = (lax.broadcasted_iota(jnp.int32, (bs, bs), 0)
                 - lax.broadcasted_iota(jnp.int32, (bs, bs), 1))
    rel = r_minus_c.astype(F32)
    nt = (((1,), (1,)), ((), ()))

    k_own = k_ref[pl.ds(pl.multiple_of(jq * bs, bs), bs), :]
    v_own = v_ref[pl.ds(pl.multiple_of(jq * bs, bs), bs), :]

    outs = []
    for hh in range(HEADS_PER_STEP):
        slope = slopes_ref[pair * HEADS_PER_STEP + hh]
        in_head = (lane >= hh * HEAD_DIM) & (lane < (hh + 1) * HEAD_DIM)
        qh = jnp.where(in_head, q, jnp.zeros_like(q))

        gate = lax.dot_general(qh.astype(F32), km_ref[...], nt,
                               precision=lax.Precision.HIGHEST,
                               preferred_element_type=F32)
        gate = jnp.where(blk < jq, gate, NEG)
        sel = jnp.zeros((bs, nb), F32)
        for _ in range(MOBA_TOPK):
            top = jnp.max(gate, axis=1, keepdims=True)
            first = jnp.min(jnp.where(gate == top, blk, nb), axis=1, keepdims=True)
            pick = blk == first
            sel = jnp.where(pick & (blk < jq), 1.0, sel)
            gate = jnp.where(pick, -jnp.inf, gate)

        s = lax.dot_general(qh, k_own, nt, preferred_element_type=F32) - slope * rel
        s = jnp.where(r_minus_c >= 0, s, NEG)
        m0 = jnp.max(s, axis=1, keepdims=True)
        p = jnp.exp(s - m0)
        l0 = jnp.sum(p, axis=1, keepdims=True)
        acc0 = jnp.dot(p.astype(BF16), v_own, preferred_element_type=F32)

        def body(n, carry, qh=qh, sel=sel, slope=slope):
            m, l, acc = carry
            start = pl.multiple_of(n * bs, bs)
            kn = k_ref[pl.ds(start, bs), :]
            vn = v_ref[pl.ds(start, bs), :]
            chosen = jnp.sum(jnp.where(blk == n, sel, 0.0), axis=1, keepdims=True) > 0.5
            base = ((jq - n) * bs).astype(F32)
            s = lax.dot_general(qh, kn, nt, preferred_element_type=F32) - slope * (rel + base)
            s = jnp.where(chosen, s, NEG)
            m_new = jnp.maximum(m, jnp.max(s, axis=1, keepdims=True))
            alpha = jnp.exp(m - m_new)
            p = jnp.exp(s - m_new)
            l = alpha * l + jnp.sum(p, axis=1, keepdims=True)
            acc = alpha * acc + jnp.dot(p.astype(BF16), vn, preferred_element_type=F32)
            return m_new, l, acc

        _, l, acc = lax.fori_loop(0, jq, body, (m0, l0, acc0))
        outs.append(acc / l)

    o_ref[...] = jnp.where(lane < HEAD_DIM, outs[0], outs[1]).astype(BF16)


def _attention(slopes, q, k, v, kmean):
    s, d = q.shape
    nb = kmean.shape[0]
    tile = pl.BlockSpec((MOBA_BLOCK, LANES), lambda p, j: (j, p))
    col = pl.BlockSpec((s, LANES), lambda p, j: (0, p))
    return pl.pallas_call(
        _attn_kernel,
        grid=(d // LANES, s // MOBA_BLOCK),
        in_specs=[pl.BlockSpec(memory_space=pltpu.SMEM), tile, col, col,
                  pl.BlockSpec((nb, LANES), lambda p, j: (0, p))],
        out_specs=tile,
        out_shape=jax.ShapeDtypeStruct((s, d), BF16),
        compiler_params=pltpu.CompilerParams(
            dimension_semantics=("arbitrary", "arbitrary"),
            vmem_limit_bytes=VMEM_LIMIT),
        name="moba_attention",
    )(slopes, q, k, v, kmean)


def _oproj_kernel(x_ref, o_ref, w_ref, y_ref):
    y_ref[...] = x_ref[...] + jnp.dot(o_ref[...], w_ref[...],
                                      preferred_element_type=F32)


def _oproj(x, o, w, *, tm=512):
    s, d = x.shape
    row = pl.BlockSpec((tm, d), lambda i: (i, 0))
    return pl.pallas_call(
        _oproj_kernel,
        grid=(s // tm,),
        in_specs=[row, row, _resident((d, d))],
        out_specs=row,
        out_shape=jax.ShapeDtypeStruct((s, d), F32),
        compiler_params=_params(),
        name="attn_out_proj",
    )(x, o, w)


def _pool_kernel(x_ref, g_ref, w_ref, sc_ref, o_ref, h_ref):
    i = pl.program_id(0)
    tm, d = x_ref.shape
    gd = d // len(POOL_WINDOWS)
    x = x_ref[...]

    @pl.when(i == 0)
    def _():
        h_ref[0:POOL_HALO, :] = jnp.zeros((POOL_HALO, d), F32)

    h_ref[POOL_HALO:, :] = _rmsnorm(x, g_ref[...])
    t = i * tm + lax.broadcasted_iota(jnp.int32, (tm, 1), 0)
    for g, w in enumerate(POOL_WINDOWS):
        cols = slice(g * gd, (g + 1) * gd)
        win = h_ref[POOL_HALO:, cols]
        for back in range(1, w):
            win = win + h_ref[POOL_HALO - back:POOL_HALO - back + tm, cols]
        cnt = jnp.minimum(t + 1, w).astype(F32)
        y = (win / cnt - h_ref[POOL_HALO:, cols]).astype(BF16)
        mixed = jnp.dot(y, w_ref[g], preferred_element_type=F32)
        o_ref[:, cols] = x[:, cols] + mixed * sc_ref[:, cols]
    h_ref[0:POOL_HALO, :] = h_ref[tm:tm + POOL_HALO, :]


def _pool(x, gain, w, scale, *, tm=512):
    s, d = x.shape
    row = pl.BlockSpec((tm, d), lambda i: (i, 0))
    return pl.pallas_call(
        _pool_kernel,
        grid=(s // tm,),
        in_specs=[row, _resident((1, d)), _resident(w.shape), _resident((1, d))],
        out_specs=row,
        out_shape=jax.ShapeDtypeStruct((s, d), F32),
        scratch_shapes=[pltpu.VMEM((tm + POOL_HALO, d), F32)],
        compiler_params=_params(),
        name="pool_mixer",
    )(x, gain, w, scale)


def kernel(x, ln_gains, ffn_w_gate, ffn_w_up, ffn_w_down, attn_w_qkv, attn_w_o,
           pool_w, pool_scale, final_gain):
    b, s, d = x.shape
    depth = ln_gains.shape[0]
    slopes = jnp.exp2(-8.0 * jnp.arange(1, N_HEADS + 1, dtype=F32) / N_HEADS)
    fg = final_gain.reshape(1, d)
    outs = []
    for bi in range(b):
        y = x[bi]
        for i in range(depth):
            gains = ln_gains[i].reshape(3, 1, d)

            def half_ffn(y, which, final_norm=False, i=i, gains=gains):
                return _ffn(y, gains[2 * which],
                            ffn_w_gate[i, which].astype(BF16),
                            ffn_w_up[i, which].astype(BF16),
                            ffn_w_down[i, which].astype(BF16),
                            fg, final_norm=final_norm)

            y = half_ffn(y, 0)
            m = i // 2
            if i % 2 == 0:
                q, k, v, kmean = _qkv(y, gains[1], attn_w_qkv[m].astype(BF16))
                o = _attention(slopes, q, k, v, kmean.reshape(-1, d))
                y = _oproj(y, o, attn_w_o[m].astype(BF16))
            else:
                y = _pool(y, gains[1], pool_w[m].astype(BF16),
                          pool_scale[m].reshape(1, d))
            y = half_ffn(y, 1, final_norm=(i == depth - 1))
        outs.append(y)
    return jnp.stack(outs)
```

```python
import functools

import jax
import jax.numpy as jnp
from jax import lax
from jax.experimental import pallas as pl
from jax.experimental.pallas import tpu as pltpu

N_HEADS = 16
HEAD_DIM = 64
MOBA_BLOCK = 256
MOBA_TOPK = 3
POOL_WINDOWS = (2, 4, 8, 16)
EPS = 1e-6
NEG = -1e30

LANES = 128
HEADS_PER_STEP = LANES // HEAD_DIM
KV_GROUP = 2
ONES_ROWS = 16
POOL_HALO = 16
VMEM_LIMIT = 56 * 1024 * 1024

F32 = jnp.float32
BF16 = jnp.bfloat16


def _params():
    return pltpu.CompilerParams(
        dimension_semantics=("arbitrary",), vmem_limit_bytes=VMEM_LIMIT)


def _resident(shape):
    return pl.BlockSpec(shape, lambda *_: (0,) * len(shape),
                        pipeline_mode=pl.Buffered(1))


def _rmsnorm(x, g):
    ms = jnp.mean(x * x, axis=-1, keepdims=True)
    return x * lax.rsqrt(ms + EPS) * g


def _ffn_kernel(x_ref, g_ref, wg_ref, wu_ref, wd_ref, fg_ref, o_ref, a_ref, *,
                ff_chunk, final_norm):
    x = x_ref[...]
    h = _rmsnorm(x, g_ref[...]).astype(BF16)
    d_ff = wg_ref.shape[1]
    for c in range(d_ff // ff_chunk):
        sl = slice(c * ff_chunk, (c + 1) * ff_chunk)
        gate = jnp.dot(h, wg_ref[:, sl], preferred_element_type=F32)
        up = jnp.dot(h, wu_ref[:, sl], preferred_element_type=F32)
        a_ref[:, sl] = (gate * jax.nn.sigmoid(gate) * up).astype(BF16)
    y = x + 0.5 * jnp.dot(a_ref[...], wd_ref[...], preferred_element_type=F32)
    if final_norm:
        y = _rmsnorm(y, fg_ref[...])
    o_ref[...] = y


def _ffn(x, gain, wg, wu, wd, final_gain, *, final_norm, tm=512, ff_chunk=256):
    s, d = x.shape
    d_ff = wg.shape[1]
    row = pl.BlockSpec((tm, d), lambda i: (i, 0))
    return pl.pallas_call(
        functools.partial(_ffn_kernel, ff_chunk=ff_chunk, final_norm=final_norm),
        grid=(s // tm,),
        in_specs=[row, _resident((1, d)), _resident((d, d_ff)),
                  _resident((d, d_ff)), _resident((d_ff, d)), _resident((1, d))],
        out_specs=row,
        out_shape=jax.ShapeDtypeStruct((s, d), F32),
        scratch_shapes=[pltpu.VMEM((tm, d_ff), BF16)],
        compiler_params=_params(),
        name="ffn",
    )(x, gain, wg, wu, wd, final_gain)


def _qkv_kernel(x_ref, g_ref, wqk_ref, wvt_ref, q_ref, k_ref, vt_ref, km_ref):
    d = x_ref.shape[1]
    h = _rmsnorm(x_ref[...], g_ref[...]).astype(BF16)
    qk = jnp.dot(h, wqk_ref[...], preferred_element_type=F32)
    q_ref[...] = (qk[:, :d] * (HEAD_DIM ** -0.5)).astype(BF16)
    k = qk[:, d:]
    k_ref[...] = k.astype(BF16)
    vt = lax.dot_general(wvt_ref[...], h, (((1,), (1,)), ((), ())),
                         preferred_element_type=F32).astype(BF16)
    for b in range(x_ref.shape[0] // MOBA_BLOCK):
        rows = slice(b * MOBA_BLOCK, (b + 1) * MOBA_BLOCK)
        km_ref[b] = jnp.mean(k[rows], axis=0, keepdims=True)
        vt_ref[b] = vt[:, rows]


def _qkv(x, gain, wqk, wvt, *, tm=512):
    s, d = x.shape
    row = pl.BlockSpec((tm, d), lambda i: (i, 0))
    nb_tile = tm // MOBA_BLOCK
    nb = s // MOBA_BLOCK
    return pl.pallas_call(
        _qkv_kernel,
        grid=(s // tm,),
        in_specs=[row, _resident((1, d)), _resident((d, 2 * d)), _resident((d, d))],
        out_specs=[row, row,
                   pl.BlockSpec((nb_tile, d, MOBA_BLOCK), lambda i: (i, 0, 0)),
                   pl.BlockSpec((nb_tile, 1, d), lambda i: (i, 0, 0))],
        out_shape=[jax.ShapeDtypeStruct((s, d), BF16)] * 2
        + [jax.ShapeDtypeStruct((nb, d, MOBA_BLOCK), BF16),
           jax.ShapeDtypeStruct((nb, 1, d), F32)],
        compiler_params=_params(),
        name="qkv",
    )(x, gain, wqk, wvt)


def _attn_kernel(slopes_ref, q_ref, k_ref, vt_ref, km_ref, o_ref,
                 bias_ref, add_ref, s0_ref, s1_ref, p0_ref, p1_ref):
    pair = pl.program_id(0)
    jq = pl.program_id(1)
    bs = MOBA_BLOCK
    grp = KV_GROUP
    nb = km_ref.shape[0]
    nt = (((1,), (1,)), ((), ()))
    heads = range(HEADS_PER_STEP)
    q = q_ref[...]
    lane = lax.broadcasted_iota(jnp.int32, (1, LANES), 1)
    rel = (lax.broadcasted_iota(jnp.int32, (bs, bs), 1)
           - lax.broadcasted_iota(jnp.int32, (bs, bs), 0))
    blk = lax.broadcasted_iota(jnp.int32, (nb, bs), 0)
    blkf = blk.astype(F32)
    past = blk < jq

    @pl.when(jq == 0)
    def _():
        for hh in heads:
            slope = slopes_ref[pair * HEADS_PER_STEP + hh]
            bias_ref[hh] = slope * rel.astype(F32)

    qhs = []
    for hh in heads:
        slope = slopes_ref[pair * HEADS_PER_STEP + hh]
        in_head = (lane >= hh * HEAD_DIM) & (lane < (hh + 1) * HEAD_DIM)
        qh = jnp.where(in_head, q, jnp.zeros_like(q))
        qhs.append(qh)
        gate = lax.dot_general(km_ref[...], qh.astype(F32), nt,
                               precision=lax.Precision.HIGHEST,
                               preferred_element_type=F32)
        gate = jnp.where(past, gate, NEG)
        sel = jnp.zeros((nb, bs), jnp.bool_)
        for _ in range(MOBA_TOPK):
            top = jnp.max(gate, axis=0, keepdims=True)
            first = jnp.min(jnp.where(gate == top, blkf, float(nb)), axis=0, keepdims=True)
            pick = blkf == first
            sel = sel | (pick & past)
            gate = jnp.where(pick, -jnp.inf, gate)
        block_bias = (-slope * bs) * (jq - blk).astype(F32)
        add_ref[hh] = jnp.where(sel, block_bias, NEG)

    ones = jnp.ones((ONES_ROWS, bs), BF16)

    def weighted_values(vt_blk, hh, p):
        lhs = jnp.concatenate([vt_blk[hh * HEAD_DIM:(hh + 1) * HEAD_DIM, :], ones], axis=0)
        return jnp.dot(lhs, p, preferred_element_type=F32)

    def group_scores(g, hh):
        keys = k_ref[pl.ds(pl.multiple_of(g * (grp * bs), grp * bs), grp * bs), :]
        return lax.dot_general(keys, qhs[hh], nt, preferred_element_type=F32)

    s_refs = (s0_ref, s1_ref)
    p_refs = (p0_ref, p1_ref)

    def group_values(g, slot, hh):
        r = None
        for b in range(grp):
            term = weighted_values(vt_ref[g * grp + b], hh,
                                   p_refs[slot][hh, b * bs:(b + 1) * bs, :])
            r = term if r is None else r + term
        return r

    for hh in heads:
        s_refs[0][hh] = group_scores(0, hh)
        p_refs[1][hh] = jnp.zeros((grp * bs, bs), BF16)

    k_own = k_ref[pl.ds(pl.multiple_of(jq * bs, bs), bs), :]
    vt_own = vt_ref[jq]
    causal = jnp.where(rel >= 0, 0.0, NEG)
    init = []
    for hh in heads:
        s = (lax.dot_general(k_own, qhs[hh], nt, preferred_element_type=F32)
             - bias_ref[hh] + causal)
        m = jnp.max(s, axis=0, keepdims=True)
        r = weighted_values(vt_own, hh, jnp.exp(s - m).astype(BF16))
        init.append((m, r[HEAD_DIM:HEAD_DIM + 1], r[:HEAD_DIM], jnp.ones_like(m)))

    def step(g, cur, carry):
        prv = 1 - cur
        g_prev = jnp.maximum(g - 1, 0)
        g_next = jnp.minimum(g + 1, nb // grp - 1)
        r_prev = [group_values(g_prev, prv, hh) for hh in heads]
        for hh in heads:
            s_refs[prv][hh] = group_scores(g_next, hh)
        out = []
        for hh in heads:
            m, l, acc, alpha_prev = carry[hh]
            sb, rows = [], []
            m_new = m
            for b in range(grp):
                sb.append(s_refs[cur][hh, b * bs:(b + 1) * bs, :] - bias_ref[hh])
                rows.append(add_ref[hh, pl.ds(g * grp + b, 1), :])
                m_new = jnp.maximum(m_new, jnp.max(sb[b], axis=0, keepdims=True) + rows[b])
            for b in range(grp):
                p_refs[cur][hh, b * bs:(b + 1) * bs, :] = jnp.exp(
                    sb[b] - (m_new - rows[b])).astype(BF16)
            r = r_prev[hh]
            out.append((m_new,
                        alpha_prev * l + r[HEAD_DIM:HEAD_DIM + 1],
                        alpha_prev * acc + r[:HEAD_DIM],
                        jnp.exp(m - m_new)))
        return tuple(out)

    span = 2 * grp
    fin = lax.fori_loop(0, (jq + span - 1) // span,
                        lambda i, c: step(2 * i + 1, 1, step(2 * i, 0, c)), tuple(init))
    g_last = jnp.maximum((jq + span - 1) // span * 2 - 1, 0)
    o_t = []
    for hh in heads:
        _, l, acc, alpha = fin[hh]
        r = group_values(g_last, 1, hh)
        o_t.append((alpha * acc + r[:HEAD_DIM]) / (alpha * l + r[HEAD_DIM:HEAD_DIM + 1]))
    o_ref[...] = jnp.concatenate(o_t, axis=0).T.astype(BF16)


def _attention(slopes, q, k, vt, kmean):
    s, d = q.shape
    nb = kmean.shape[0]
    tile = pl.BlockSpec((MOBA_BLOCK, LANES), lambda p, j: (j, p))
    group_keys = KV_GROUP * MOBA_BLOCK
    return pl.pallas_call(
        _attn_kernel,
        grid=(d // LANES, s // MOBA_BLOCK),
        in_specs=[pl.BlockSpec(memory_space=pltpu.SMEM), tile,
                  pl.BlockSpec((s, LANES), lambda p, j: (0, p)),
                  pl.BlockSpec((nb, LANES, MOBA_BLOCK), lambda p, j: (0, p, 0)),
                  pl.BlockSpec((nb, LANES), lambda p, j: (0, p))],
        out_specs=tile,
        out_shape=jax.ShapeDtypeStruct((s, d), BF16),
        scratch_shapes=[pltpu.VMEM((HEADS_PER_STEP, MOBA_BLOCK, MOBA_BLOCK), F32),
                        pltpu.VMEM((HEADS_PER_STEP, nb, MOBA_BLOCK), F32),
                        pltpu.VMEM((HEADS_PER_STEP, group_keys, MOBA_BLOCK), F32),
                        pltpu.VMEM((HEADS_PER_STEP, group_keys, MOBA_BLOCK), F32),
                        pltpu.VMEM((HEADS_PER_STEP, group_keys, MOBA_BLOCK), BF16),
                        pltpu.VMEM((HEADS_PER_STEP, group_keys, MOBA_BLOCK), BF16)],
        compiler_params=pltpu.CompilerParams(
            dimension_semantics=("arbitrary", "arbitrary"),
            vmem_limit_bytes=VMEM_LIMIT),
        name="moba_attention",
    )(slopes, q, k, vt, kmean)


def _oproj_kernel(x_ref, o_ref, w_ref, y_ref):
    y_ref[...] = x_ref[...] + jnp.dot(o_ref[...], w_ref[...],
                                      preferred_element_type=F32)


def _oproj(x, o, w, *, tm=512):
    s, d = x.shape
    row = pl.BlockSpec((tm, d), lambda i: (i, 0))
    return pl.pallas_call(
        _oproj_kernel,
        grid=(s // tm,),
        in_specs=[row, row, _resident((d, d))],
        out_specs=row,
        out_shape=jax.ShapeDtypeStruct((s, d), F32),
        compiler_params=_params(),
        name="attn_out_proj",
    )(x, o, w)


def _pool_kernel(x_ref, g_ref, w_ref, sc_ref, o_ref, h_ref):
    i = pl.program_id(0)
    tm, d = x_ref.shape
    gd = d // len(POOL_WINDOWS)
    x = x_ref[...]

    @pl.when(i == 0)
    def _():
        h_ref[0:POOL_HALO, :] = jnp.zeros((POOL_HALO, d), F32)

    h_ref[POOL_HALO:, :] = _rmsnorm(x, g_ref[...])
    t = i * tm + lax.broadcasted_iota(jnp.int32, (tm, 1), 0)
    for g, w in enumerate(POOL_WINDOWS):
        cols = slice(g * gd, (g + 1) * gd)
        win = h_ref[POOL_HALO:, cols]
        for back in range(1, w):
            win = win + h_ref[POOL_HALO - back:POOL_HALO - back + tm, cols]
        cnt = jnp.minimum(t + 1, w).astype(F32)
        y = (win / cnt - h_ref[POOL_HALO:, cols]).astype(BF16)
        mixed = jnp.dot(y, w_ref[g], preferred_element_type=F32)
        o_ref[:, cols] = x[:, cols] + mixed * sc_ref[:, cols]
    h_ref[0:POOL_HALO, :] = h_ref[tm:tm + POOL_HALO, :]


def _pool(x, gain, w, scale, *, tm=512):
    s, d = x.shape
    row = pl.BlockSpec((tm, d), lambda i: (i, 0))
    return pl.pallas_call(
        _pool_kernel,
        grid=(s // tm,),
        in_specs=[row, _resident((1, d)), _resident(w.shape), _resident((1, d))],
        out_specs=row,
        out_shape=jax.ShapeDtypeStruct((s, d), F32),
        scratch_shapes=[pltpu.VMEM((tm + POOL_HALO, d), F32)],
        compiler_params=_params(),
        name="pool_mixer",
    )(x, gain, w, scale)


def kernel(x, ln_gains, ffn_w_gate, ffn_w_up, ffn_w_down, attn_w_qkv, attn_w_o,
           pool_w, pool_scale, final_gain):
    b, s, d = x.shape
    assert b == 1
    depth = ln_gains.shape[0]
    slopes = jnp.exp2(-8.0 * jnp.arange(1, N_HEADS + 1, dtype=F32) / N_HEADS)
    fg = final_gain.reshape(1, d)
    y = x.reshape(s, d)
    for i in range(depth):
        gains = ln_gains[i].reshape(3, 1, d)

        def half_ffn(y, which, final_norm=False, i=i, gains=gains):
            return _ffn(y, gains[2 * which],
                        ffn_w_gate[i, which].astype(BF16),
                        ffn_w_up[i, which].astype(BF16),
                        ffn_w_down[i, which].astype(BF16),
                        fg, final_norm=final_norm)

        y = half_ffn(y, 0)
        m = i // 2
        if i % 2 == 0:
            w_qkv = attn_w_qkv[m]
            q, k, vt, kmean = _qkv(y, gains[1], w_qkv[:, :2 * d].astype(BF16),
                                   w_qkv[:, 2 * d:].T.astype(BF16))
            o = _attention(slopes, q, k, vt, kmean.reshape(-1, d))
            y = _oproj(y, o, attn_w_o[m].astype(BF16))
        else:
            y = _pool(y, gains[1], pool_w[m].astype(BF16),
                      pool_scale[m].reshape(1, d))
        y = half_ffn(y, 1, final_norm=(i == depth - 1))
    return y.reshape(b, s, d)
```

```python
import functools

import jax
import jax.numpy as jnp
from jax import lax
from jax.experimental import pallas as pl
from jax.experimental.pallas import tpu as pltpu

N_HEADS = 16
HEAD_DIM = 64
MOBA_BLOCK = 256
MOBA_TOPK = 3
POOL_WINDOWS = (2, 4, 8, 16)
EPS = 1e-6
NEG = -1e30
LOG2E = 1.4426950408889634

LANES = 128
HEADS_PER_STEP = LANES // HEAD_DIM
ONES_ROWS = 16
POOL_HALO = 16
VMEM_LIMIT = 56 * 1024 * 1024

F32 = jnp.float32
BF16 = jnp.bfloat16


def _params():
    return pltpu.CompilerParams(
        dimension_semantics=("arbitrary",), vmem_limit_bytes=VMEM_LIMIT)


def _resident(shape):
    return pl.BlockSpec(shape, lambda *_: (0,) * len(shape),
                        pipeline_mode=pl.Buffered(1))


def _rmsnorm(x, g):
    ms = jnp.mean(x * x, axis=-1, keepdims=True)
    return x * lax.rsqrt(ms + EPS) * g


def _ffn_kernel(x_ref, g_ref, wg_ref, wu_ref, wd_ref, fg_ref, o_ref, a_ref, *,
                ff_chunk, final_norm):
    x = x_ref[...]
    h = _rmsnorm(x, g_ref[...]).astype(BF16)
    d_ff = wg_ref.shape[1]
    for c in range(d_ff // ff_chunk):
        sl = slice(c * ff_chunk, (c + 1) * ff_chunk)
        gate = jnp.dot(h, wg_ref[:, sl], preferred_element_type=F32)
        up = jnp.dot(h, wu_ref[:, sl], preferred_element_type=F32)
        a_ref[:, sl] = (gate * jax.nn.sigmoid(gate) * up).astype(BF16)
    y = x + 0.5 * jnp.dot(a_ref[...], wd_ref[...], preferred_element_type=F32)
    if final_norm:
        y = _rmsnorm(y, fg_ref[...])
    o_ref[...] = y


def _ffn(x, gain, wg, wu, wd, final_gain, *, final_norm, tm=512, ff_chunk=256):
    s, d = x.shape
    d_ff = wg.shape[1]
    row = pl.BlockSpec((tm, d), lambda i: (i, 0))
    return pl.pallas_call(
        functools.partial(_ffn_kernel, ff_chunk=ff_chunk, final_norm=final_norm),
        grid=(s // tm,),
        in_specs=[row, _resident((1, d)), _resident((d, d_ff)),
                  _resident((d, d_ff)), _resident((d_ff, d)), _resident((1, d))],
        out_specs=row,
        out_shape=jax.ShapeDtypeStruct((s, d), F32),
        scratch_shapes=[pltpu.VMEM((tm, d_ff), BF16)],
        compiler_params=_params(),
        name="ffn",
    )(x, gain, wg, wu, wd, final_gain)


def _qkv_kernel(x_ref, g_ref, wqk_ref, wvt_ref, q_ref, k_ref, vt_ref, km_ref):
    d = x_ref.shape[1]
    h = _rmsnorm(x_ref[...], g_ref[...]).astype(BF16)
    qk = jnp.dot(h, wqk_ref[...], preferred_element_type=F32)
    q_ref[...] = (qk[:, :d] * (HEAD_DIM ** -0.5 * LOG2E)).astype(BF16)
    k = qk[:, d:]
    k_ref[...] = k.astype(BF16)
    vt = lax.dot_general(wvt_ref[...], h, (((1,), (1,)), ((), ())),
                         preferred_element_type=F32).astype(BF16)
    for b in range(x_ref.shape[0] // MOBA_BLOCK):
        rows = slice(b * MOBA_BLOCK, (b + 1) * MOBA_BLOCK)
        km_ref[b] = jnp.mean(k[rows], axis=0, keepdims=True)
        vt_ref[b] = vt[:, rows]


def _qkv(x, gain, wqk, wvt, *, tm=512):
    s, d = x.shape
    row = pl.BlockSpec((tm, d), lambda i: (i, 0))
    nb_tile = tm // MOBA_BLOCK
    nb = s // MOBA_BLOCK
    return pl.pallas_call(
        _qkv_kernel,
        grid=(s // tm,),
        in_specs=[row, _resident((1, d)), _resident((d, 2 * d)), _resident((d, d))],
        out_specs=[row, row,
                   pl.BlockSpec((nb_tile, d, MOBA_BLOCK), lambda i: (i, 0, 0)),
                   pl.BlockSpec((nb_tile, 1, d), lambda i: (i, 0, 0))],
        out_shape=[jax.ShapeDtypeStruct((s, d), BF16)] * 2
        + [jax.ShapeDtypeStruct((nb, d, MOBA_BLOCK), BF16),
           jax.ShapeDtypeStruct((nb, 1, d), F32)],
        compiler_params=_params(),
        name="qkv",
    )(x, gain, wqk, wvt)


def _attn_kernel(slopes_ref, q_ref, k_ref, vt_ref, km_ref, o_ref,
                 bias_ref, add_ref, s0_ref, s1_ref, p0_ref, p1_ref):
    pair = pl.program_id(0)
    jq = pl.program_id(1)
    bs = MOBA_BLOCK
    nb = km_ref.shape[0]
    nt = (((1,), (1,)), ((), ()))
    heads = range(HEADS_PER_STEP)
    halves = [slice(h * LANES, (h + 1) * LANES) for h in range(bs // LANES)]
    s_refs = (s0_ref, s1_ref)
    p_refs = (p0_ref, p1_ref)
    q = q_ref[...]
    lane = lax.broadcasted_iota(jnp.int32, (1, LANES), 1)
    rel = (lax.broadcasted_iota(jnp.int32, (bs, bs), 1)
           - lax.broadcasted_iota(jnp.int32, (bs, bs), 0))
    blk = lax.broadcasted_iota(jnp.int32, (nb, bs), 0)
    blkf = blk.astype(F32)
    past = blk < jq

    @pl.when(jq == 0)
    def _():
        for hh in heads:
            bias_ref[hh] = slopes_ref[pair * HEADS_PER_STEP + hh] * rel.astype(F32)

    qhs = []
    for hh in heads:
        in_head = (lane >= hh * HEAD_DIM) & (lane < (hh + 1) * HEAD_DIM)
        qhs.append(jnp.where(in_head, q, jnp.zeros_like(q)))

    def block_scores(n, hh):
        keys = k_ref[pl.ds(pl.multiple_of(n * bs, bs), bs), :]
        return lax.dot_general(keys, qhs[hh], nt, preferred_element_type=F32)

    ones = jnp.ones((ONES_ROWS, bs), BF16)

    def weighted_values(n, slot, hh):
        lhs = jnp.concatenate([vt_ref[n][hh * HEAD_DIM:(hh + 1) * HEAD_DIM, :], ones], axis=0)
        return jnp.dot(lhs, p_refs[slot][hh], preferred_element_type=F32)

    s_own = [block_scores(jq, hh) for hh in heads]
    for hh in heads:
        s_refs[0][hh] = block_scores(0, hh)

    km = km_ref[...]
    km_hi = km.astype(BF16)
    rest = km - km_hi.astype(F32)
    km_mid = rest.astype(BF16)
    km_lo = (rest - km_mid.astype(F32)).astype(BF16)
    km_terms = jnp.concatenate([km_hi, km_mid, km_lo], axis=0)
    for hh in heads:
        g3 = lax.dot_general(km_terms, qhs[hh], nt, preferred_element_type=F32)
        gate = g3[:nb] + g3[nb:2 * nb] + g3[2 * nb:]
        gate = jnp.where(past, gate, NEG)
        sel = jnp.zeros((nb, bs), jnp.bool_)
        for _ in range(MOBA_TOPK):
            top = jnp.max(gate, axis=0, keepdims=True)
            first = jnp.min(jnp.where(gate == top, blkf, float(nb)), axis=0, keepdims=True)
            pick = blkf == first
            sel = sel | (pick & past)
            gate = jnp.where(pick, -jnp.inf, gate)
        block_bias = (-slopes_ref[pair * HEADS_PER_STEP + hh] * bs) * (jq - blk).astype(F32)
        add_ref[hh] = jnp.where(sel, block_bias, NEG)

    causal = jnp.where(rel >= 0, 0.0, NEG)
    init = []
    for hh in heads:
        m = []
        for cols in halves:
            sb = s_own[hh][:, cols] - bias_ref[hh, :, cols] + causal[:, cols]
            m.append(jnp.max(sb, axis=0, keepdims=True))
            p_refs[1][hh, :, cols] = jnp.exp2(sb - m[-1]).astype(BF16)
        init.append((tuple(m), jnp.zeros((1, bs), F32), jnp.zeros((HEAD_DIM, bs), F32),
                     jnp.ones((1, bs), F32)))

    def step(n, n_prev, cur, carry):
        prv = 1 - cur
        folded = []
        for hh in heads:
            _, l, acc, alpha_prev = carry[hh]
            r = weighted_values(n_prev, prv, hh)
            folded.append((alpha_prev * l + r[HEAD_DIM:HEAD_DIM + 1],
                           alpha_prev * acc + r[:HEAD_DIM]))
        n_next = jnp.minimum(n + 1, nb - 1)
        for hh in heads:
            s_refs[prv][hh] = block_scores(n_next, hh)
        out = []
        for hh in heads:
            m = carry[hh][0]
            row = add_ref[hh, pl.ds(n, 1), :]
            m_new, alpha = [], []
            for h, cols in enumerate(halves):
                sb = s_refs[cur][hh, :, cols] - bias_ref[hh, :, cols]
                top = jnp.max(sb, axis=0, keepdims=True) + row[:, cols]
                m_new.append(jnp.maximum(m[h], top))
                p_refs[cur][hh, :, cols] = jnp.exp2(sb - (m_new[h] - row[:, cols])).astype(BF16)
                alpha.append(jnp.exp2(m[h] - m_new[h]))
            out.append((tuple(m_new), *folded[hh], jnp.concatenate(alpha, axis=1)))
        return tuple(out)

    def body(i, carry):
        carry = step(2 * i, jnp.where(i == 0, jq, 2 * i - 1), 0, carry)
        return step(2 * i + 1, 2 * i, 1, carry)

    n_iter = (jq + 1) // 2
    fin = lax.fori_loop(0, n_iter, body, tuple(init))
    n_last = jnp.where(n_iter == 0, jq, 2 * n_iter - 1)
    o_t = []
    for hh in heads:
        _, l, acc, alpha = fin[hh]
        r = weighted_values(n_last, 1, hh)
        o_t.append((alpha * acc + r[:HEAD_DIM]) / (alpha * l + r[HEAD_DIM:HEAD_DIM + 1]))
    o_ref[...] = jnp.concatenate(o_t, axis=0).T.astype(BF16)


def _attention(slopes, q, k, vt, kmean):
    s, d = q.shape
    nb = kmean.shape[0]
    tile = pl.BlockSpec((MOBA_BLOCK, LANES), lambda p, j: (j, p))
    return pl.pallas_call(
        _attn_kernel,
        grid=(d // LANES, s // MOBA_BLOCK),
        in_specs=[pl.BlockSpec(memory_space=pltpu.SMEM), tile,
                  pl.BlockSpec((s, LANES), lambda p, j: (0, p)),
                  pl.BlockSpec((nb, LANES, MOBA_BLOCK), lambda p, j: (0, p, 0)),
                  pl.BlockSpec((nb, LANES), lambda p, j: (0, p))],
        out_specs=tile,
        out_shape=jax.ShapeDtypeStruct((s, d), BF16),
        scratch_shapes=[pltpu.VMEM((HEADS_PER_STEP, MOBA_BLOCK, MOBA_BLOCK), F32),
                        pltpu.VMEM((HEADS_PER_STEP, nb, MOBA_BLOCK), F32),
                        pltpu.VMEM((HEADS_PER_STEP, MOBA_BLOCK, MOBA_BLOCK), F32),
                        pltpu.VMEM((HEADS_PER_STEP, MOBA_BLOCK, MOBA_BLOCK), F32),
                        pltpu.VMEM((HEADS_PER_STEP, MOBA_BLOCK, MOBA_BLOCK), BF16),
                        pltpu.VMEM((HEADS_PER_STEP, MOBA_BLOCK, MOBA_BLOCK), BF16)],
        compiler_params=pltpu.CompilerParams(
            dimension_semantics=("arbitrary", "arbitrary"),
            vmem_limit_bytes=VMEM_LIMIT),
        name="moba_attention",
    )(slopes, q, k, vt, kmean)


def _oproj_kernel(x_ref, o_ref, w_ref, y_ref):
    y_ref[...] = x_ref[...] + jnp.dot(o_ref[...], w_ref[...],
                                      preferred_element_type=F32)


def _oproj(x, o, w, *, tm=512):
    s, d = x.shape
    row = pl.BlockSpec((tm, d), lambda i: (i, 0))
    return pl.pallas_call(
        _oproj_kernel,
        grid=(s // tm,),
        in_specs=[row, row, _resident((d, d))],
        out_specs=row,
        out_shape=jax.ShapeDtypeStruct((s, d), F32),
        compiler_params=_params(),
        name="attn_out_proj",
    )(x, o, w)


def _pool_kernel(x_ref, g_ref, w_ref, sc_ref, o_ref, h_ref):
    i = pl.program_id(0)
    tm, d = x_ref.shape
    gd = d // len(POOL_WINDOWS)
    x = x_ref[...]

    @pl.when(i == 0)
    def _():
        h_ref[0:POOL_HALO, :] = jnp.zeros((POOL_HALO, d), F32)

    h_ref[POOL_HALO:, :] = _rmsnorm(x, g_ref[...])
    t = i * tm + lax.broadcasted_iota(jnp.int32, (tm, 1), 0)
    for g, w in enumerate(POOL_WINDOWS):
        cols = slice(g * gd, (g + 1) * gd)
        win = h_ref[POOL_HALO:, cols]
        for back in range(1, w):
            win = win + h_ref[POOL_HALO - back:POOL_HALO - back + tm, cols]
        cnt = jnp.minimum(t + 1, w).astype(F32)
        y = (win / cnt - h_ref[POOL_HALO:, cols]).astype(BF16)
        mixed = jnp.dot(y, w_ref[g], preferred_element_type=F32)
        o_ref[:, cols] = x[:, cols] + mixed * sc_ref[:, cols]
    h_ref[0:POOL_HALO, :] = h_ref[tm:tm + POOL_HALO, :]


def _pool(x, gain, w, scale, *, tm=512):
    s, d = x.shape
    row = pl.BlockSpec((tm, d), lambda i: (i, 0))
    return pl.pallas_call(
        _pool_kernel,
        grid=(s // tm,),
        in_specs=[row, _resident((1, d)), _resident(w.shape), _resident((1, d))],
        out_specs=row,
        out_shape=jax.ShapeDtypeStruct((s, d), F32),
        scratch_shapes=[pltpu.VMEM((tm + POOL_HALO, d), F32)],
        compiler_params=_params(),
        name="pool_mixer",
    )(x, gain, w, scale)


def kernel(x, ln_gains, ffn_w_gate, ffn_w_up, ffn_w_down, attn_w_qkv, attn_w_o,
           pool_w, pool_scale, final_gain):
    b, s, d = x.shape
    assert b == 1
    depth = ln_gains.shape[0]
    slopes = LOG2E * jnp.exp2(-8.0 * jnp.arange(1, N_HEADS + 1, dtype=F32) / N_HEADS)
    fg = final_gain.reshape(1, d)
    y = x.reshape(s, d)
    for i in range(depth):
        gains = ln_gains[i].reshape(3, 1, d)

        def half_ffn(y, which, final_norm=False, i=i, gains=gains):
            return _ffn(y, gains[2 * which],
                        ffn_w_gate[i, which].astype(BF16),
                        ffn_w_up[i, which].astype(BF16),
                        ffn_w_down[i, which].astype(BF16),
                        fg, final_norm=final_norm)

        y = half_ffn(y, 0)
        m = i // 2
        if i % 2 == 0:
            w_qkv = attn_w_qkv[m]
            q, k, vt, kmean = _qkv(y, gains[1], w_qkv[:, :2 * d].astype(BF16),
                                   w_qkv[:, 2 * d:].T.astype(BF16))
            o = _attention(slopes, q, k, vt, kmean.reshape(-1, d))
            y = _oproj(y, o, attn_w_o[m].astype(BF16))
        else:
            y = _pool(y, gains[1], pool_w[m].astype(BF16),
                      pool_scale[m].reshape(1, d))
        y = half_ffn(y, 1, final_norm=(i == depth - 1))
    return y.reshape(b, s, d)
```

```python
import functools

import jax
import jax.numpy as jnp
from jax import lax
from jax.experimental import pallas as pl
from jax.experimental.pallas import tpu as pltpu

N_HEADS = 16
HEAD_DIM = 64
MOBA_BLOCK = 256
MOBA_TOPK = 3
POOL_WINDOWS = (2, 4, 8, 16)
EPS = 1e-6
NEG = -1e30
LOG2E = 1.4426950408889634
UNDERFLOW = 160.0
NORM_SLACK = 1.02
NORM_CHUNK = 2048

LANES = 128
HEADS_PER_STEP = LANES // HEAD_DIM
ONES_ROWS = 16
POOL_HALO = 16
VMEM_LIMIT = 56 * 1024 * 1024

F32 = jnp.float32
BF16 = jnp.bfloat16


def _params():
    return pltpu.CompilerParams(
        dimension_semantics=("arbitrary",), vmem_limit_bytes=VMEM_LIMIT)


def _resident(shape):
    return pl.BlockSpec(shape, lambda *_: (0,) * len(shape),
                        pipeline_mode=pl.Buffered(1))


def _rmsnorm(x, g):
    ms = jnp.mean(x * x, axis=-1, keepdims=True)
    return x * lax.rsqrt(ms + EPS) * g


def _ffn_kernel(x_ref, g_ref, wg_ref, wu_ref, wd_ref, fg_ref, o_ref, a_ref, *,
                ff_chunk, final_norm):
    x = x_ref[...]
    h = _rmsnorm(x, g_ref[...]).astype(BF16)
    d_ff = wg_ref.shape[1]
    for c in range(d_ff // ff_chunk):
        sl = slice(c * ff_chunk, (c + 1) * ff_chunk)
        gate = jnp.dot(h, wg_ref[:, sl], preferred_element_type=F32)
        up = jnp.dot(h, wu_ref[:, sl], preferred_element_type=F32)
        a_ref[:, sl] = (gate * jax.nn.sigmoid(gate) * up).astype(BF16)
    y = x + 0.5 * jnp.dot(a_ref[...], wd_ref[...], preferred_element_type=F32)
    if final_norm:
        y = _rmsnorm(y, fg_ref[...])
    o_ref[...] = y


def _ffn(x, gain, wg, wu, wd, final_gain, *, final_norm, tm=512, ff_chunk=256):
    s, d = x.shape
    d_ff = wg.shape[1]
    row = pl.BlockSpec((tm, d), lambda i: (i, 0))
    return pl.pallas_call(
        functools.partial(_ffn_kernel, ff_chunk=ff_chunk, final_norm=final_norm),
        grid=(s // tm,),
        in_specs=[row, _resident((1, d)), _resident((d, d_ff)),
                  _resident((d, d_ff)), _resident((d_ff, d)), _resident((1, d))],
        out_specs=row,
        out_shape=jax.ShapeDtypeStruct((s, d), F32),
        scratch_shapes=[pltpu.VMEM((tm, d_ff), BF16)],
        compiler_params=_params(),
        name="ffn",
    )(x, gain, wg, wu, wd, final_gain)


def _qkv_kernel(x_ref, g_ref, wqk_ref, wvt_ref, q_ref, k_ref, vt_ref, km_ref):
    d = x_ref.shape[1]
    h = _rmsnorm(x_ref[...], g_ref[...]).astype(BF16)
    qk = jnp.dot(h, wqk_ref[...], preferred_element_type=F32)
    q_ref[...] = (qk[:, :d] * (HEAD_DIM ** -0.5 * LOG2E)).astype(BF16)
    k = qk[:, d:]
    k_ref[...] = k.astype(BF16)
    vt = lax.dot_general(wvt_ref[...], h, (((1,), (1,)), ((), ())),
                         preferred_element_type=F32).astype(BF16)
    for b in range(x_ref.shape[0] // MOBA_BLOCK):
        rows = slice(b * MOBA_BLOCK, (b + 1) * MOBA_BLOCK)
        km_ref[b] = jnp.mean(k[rows], axis=0, keepdims=True)
        vt_ref[b] = vt[:, rows]


def _qkv(x, gain, wqk, wvt, *, tm=512):
    s, d = x.shape
    row = pl.BlockSpec((tm, d), lambda i: (i, 0))
    nb_tile = tm // MOBA_BLOCK
    nb = s // MOBA_BLOCK
    return pl.pallas_call(
        _qkv_kernel,
        grid=(s // tm,),
        in_specs=[row, _resident((1, d)), _resident((d, 2 * d)), _resident((d, d))],
        out_specs=[row, row,
                   pl.BlockSpec((nb_tile, d, MOBA_BLOCK), lambda i: (i, 0, 0)),
                   pl.BlockSpec((nb_tile, 1, d), lambda i: (i, 0, 0))],
        out_shape=[jax.ShapeDtypeStruct((s, d), BF16)] * 2
        + [jax.ShapeDtypeStruct((nb, d, MOBA_BLOCK), BF16),
           jax.ShapeDtypeStruct((nb, 1, d), F32)],
        compiler_params=_params(),
        name="qkv",
    )(x, gain, wqk, wvt)


def _attn_kernel(slopes_ref, q_ref, k_ref, vt_ref, km_ref, o_ref,
                 bias_ref, add_ref, kn_ref, s0_ref, s1_ref, p0_ref, p1_ref):
    pair = pl.program_id(0)
    jq = pl.program_id(1)
    bs = MOBA_BLOCK
    nb = km_ref.shape[0]
    nt = (((1,), (1,)), ((), ()))
    heads = range(HEADS_PER_STEP)
    halves = [slice(h * LANES, (h + 1) * LANES) for h in range(bs // LANES)]
    s_refs = (s0_ref, s1_ref)
    p_refs = (p0_ref, p1_ref)
    q = q_ref[...]
    lane = lax.broadcasted_iota(jnp.int32, (1, LANES), 1)
    rel = (lax.broadcasted_iota(jnp.int32, (bs, bs), 1)
           - lax.broadcasted_iota(jnp.int32, (bs, bs), 0))
    blk = lax.broadcasted_iota(jnp.int32, (nb, bs), 0)
    blkf = blk.astype(F32)
    past = blk < jq

    lane_head = (lax.broadcasted_iota(jnp.int32, (LANES, LANES), 0) // HEAD_DIM
                 == lax.broadcasted_iota(jnp.int32, (LANES, LANES), 1)).astype(BF16)

    def max_head_norm_sq(x):
        xf = x.astype(F32)
        sq = jnp.dot((xf * xf).astype(BF16), lane_head, preferred_element_type=F32)
        return jnp.max(sq, axis=0, keepdims=True)

    @pl.when(jq == 0)
    def _():
        for hh in heads:
            bias_ref[hh] = slopes_ref[pair * HEADS_PER_STEP + hh] * rel.astype(F32)
        kn = jnp.zeros((1, LANES), F32)
        for c in range(k_ref.shape[0] // NORM_CHUNK):
            kn = jnp.maximum(kn, max_head_norm_sq(k_ref[c * NORM_CHUNK:(c + 1) * NORM_CHUNK, :]))
        kn_ref[...] = kn

    qhs = []
    for hh in heads:
        in_head = (lane >= hh * HEAD_DIM) & (lane < (hh + 1) * HEAD_DIM)
        qhs.append(jnp.where(in_head, q, jnp.zeros_like(q)))

    def block_scores(n, hh):
        keys = k_ref[pl.ds(pl.multiple_of(n * bs, bs), bs), :]
        return lax.dot_general(keys, qhs[hh], nt, preferred_element_type=F32)

    ones = jnp.ones((ONES_ROWS, bs), BF16)

    def weighted_values(n, slot, hh):
        lhs = jnp.concatenate([vt_ref[n][hh * HEAD_DIM:(hh + 1) * HEAD_DIM, :], ones], axis=0)
        return jnp.dot(lhs, p_refs[slot][hh], preferred_element_type=F32)

    causal = jnp.where(rel >= 0, 0.0, NEG)
    init = []
    for hh in heads:
        s_own = block_scores(jq, hh)
        m = []
        for cols in halves:
            sb = s_own[:, cols] - bias_ref[hh, :, cols] + causal[:, cols]
            m.append(jnp.max(sb, axis=0, keepdims=True))
            p_refs[1][hh, :, cols] = jnp.exp2(sb - m[-1]).astype(BF16)
        init.append((tuple(m), jnp.zeros((1, bs), F32), jnp.zeros((HEAD_DIM, bs), F32),
                     jnp.ones((1, bs), F32)))

    qk_bound = jnp.sqrt(max_head_norm_sq(q) * kn_ref[...] * NORM_SLACK)
    first_block = None
    for hh in heads:
        m_min = jnp.min(jnp.minimum(*init[hh][0]), axis=1, keepdims=True)
        reach = (UNDERFLOW + qk_bound[:, hh:hh + 1] - m_min) / slopes_ref[pair * HEADS_PER_STEP + hh]
        near = jq.astype(F32) - jnp.floor((reach - 1.0) / bs) - 1.0
        near = jnp.where(near > 0.0, near, 0.0)
        first_block = near if first_block is None else jnp.minimum(first_block, near)
    i_start = first_block[0, 0].astype(jnp.int32) // 2

    for hh in heads:
        s_refs[0][hh] = block_scores(2 * i_start, hh)

    km = km_ref[...]
    km_hi = km.astype(BF16)
    rest = km - km_hi.astype(F32)
    km_mid = rest.astype(BF16)
    km_lo = (rest - km_mid.astype(F32)).astype(BF16)
    km_terms = jnp.concatenate([km_hi, km_mid, km_lo], axis=0)
    for hh in heads:
        g3 = lax.dot_general(km_terms, qhs[hh], nt, preferred_element_type=F32)
        gate = g3[:nb] + g3[nb:2 * nb] + g3[2 * nb:]
        gate = jnp.where(past, gate, NEG)
        sel = jnp.zeros((nb, bs), jnp.bool_)
        for _ in range(MOBA_TOPK):
            top = jnp.max(gate, axis=0, keepdims=True)
            first = jnp.min(jnp.where(gate == top, blkf, float(nb)), axis=0, keepdims=True)
            pick = blkf == first
            sel = sel | (pick & past)
            gate = jnp.where(pick, -jnp.inf, gate)
        block_bias = (-slopes_ref[pair * HEADS_PER_STEP + hh] * bs) * (jq - blk).astype(F32)
        add_ref[hh] = jnp.where(sel, block_bias, NEG)

    def step(n, n_prev, cur, carry):
        prv = 1 - cur
        folded = []
        for hh in heads:
            _, l, acc, alpha_prev = carry[hh]
            r = weighted_values(n_prev, prv, hh)
            folded.append((alpha_prev * l + r[HEAD_DIM:HEAD_DIM + 1],
                           alpha_prev * acc + r[:HEAD_DIM]))
        n_next = jnp.minimum(n + 1, nb - 1)
        for hh in heads:
            s_refs[prv][hh] = block_scores(n_next, hh)
        out = []
        for hh in heads:
            m = carry[hh][0]
            row = add_ref[hh, pl.ds(n, 1), :]
            m_new, alpha = [], []
            for h, cols in enumerate(halves):
                sb = s_refs[cur][hh, :, cols] - bias_ref[hh, :, cols]
                top = jnp.max(sb, axis=0, keepdims=True) + row[:, cols]
                m_new.append(jnp.maximum(m[h], top))
                p_refs[cur][hh, :, cols] = jnp.exp2(sb - (m_new[h] - row[:, cols])).astype(BF16)
                alpha.append(jnp.exp2(m[h] - m_new[h]))
            out.append((tuple(m_new), *folded[hh], jnp.concatenate(alpha, axis=1)))
        return tuple(out)

    def body(i, carry):
        carry = step(2 * i, jnp.where(i == i_start, jq, 2 * i - 1), 0, carry)
        return step(2 * i + 1, 2 * i, 1, carry)

    i_stop = (jq + 1) // 2
    fin = lax.fori_loop(i_start, i_stop, body, tuple(init))
    n_last = jnp.where(i_stop == i_start, jq, 2 * i_stop - 1)
    o_t = []
    for hh in heads:
        _, l, acc, alpha = fin[hh]
        r = weighted_values(n_last, 1, hh)
        o_t.append((alpha * acc + r[:HEAD_DIM]) / (alpha * l + r[HEAD_DIM:HEAD_DIM + 1]))
    o_ref[...] = jnp.concatenate(o_t, axis=0).T.astype(BF16)


def _attention(slopes, q, k, vt, kmean):
    s, d = q.shape
    nb = kmean.shape[0]
    tile = pl.BlockSpec((MOBA_BLOCK, LANES), lambda p, j: (j, p))
    return pl.pallas_call(
        _attn_kernel,
        grid=(d // LANES, s // MOBA_BLOCK),
        in_specs=[pl.BlockSpec(memory_space=pltpu.SMEM), tile,
                  pl.BlockSpec((s, LANES), lambda p, j: (0, p)),
                  pl.BlockSpec((nb, LANES, MOBA_BLOCK), lambda p, j: (0, p, 0)),
                  pl.BlockSpec((nb, LANES), lambda p, j: (0, p))],
        out_specs=tile,
        out_shape=jax.ShapeDtypeStruct((s, d), BF16),
        scratch_shapes=[pltpu.VMEM((HEADS_PER_STEP, MOBA_BLOCK, MOBA_BLOCK), F32),
                        pltpu.VMEM((HEADS_PER_STEP, nb, MOBA_BLOCK), F32),
                        pltpu.VMEM((1, LANES), F32),
                        pltpu.VMEM((HEADS_PER_STEP, MOBA_BLOCK, MOBA_BLOCK), F32),
                        pltpu.VMEM((HEADS_PER_STEP, MOBA_BLOCK, MOBA_BLOCK), F32),
                        pltpu.VMEM((HEADS_PER_STEP, MOBA_BLOCK, MOBA_BLOCK), BF16),
                        pltpu.VMEM((HEADS_PER_STEP, MOBA_BLOCK, MOBA_BLOCK), BF16)],
        compiler_params=pltpu.CompilerParams(
            dimension_semantics=("arbitrary", "arbitrary"),
            vmem_limit_bytes=VMEM_LIMIT),
        name="moba_attention",
    )(slopes, q, k, vt, kmean)


def _oproj_kernel(x_ref, o_ref, w_ref, y_ref):
    y_ref[...] = x_ref[...] + jnp.dot(o_ref[...], w_ref[...],
                                      preferred_element_type=F32)


def _oproj(x, o, w, *, tm=512):
    s, d = x.shape
    row = pl.BlockSpec((tm, d), lambda i: (i, 0))
    return pl.pallas_call(
        _oproj_kernel,
        grid=(s // tm,),
        in_specs=[row, row, _resident((d, d))],
        out_specs=row,
        out_shape=jax.ShapeDtypeStruct((s, d), F32),
        compiler_params=_params(),
        name="attn_out_proj",
    )(x, o, w)


def _pool_kernel(x_ref, g_ref, w_ref, sc_ref, o_ref, h_ref):
    i = pl.program_id(0)
    tm, d = x_ref.shape
    gd = d // len(POOL_WINDOWS)
    x = x_ref[...]

    @pl.when(i == 0)
    def _():
        h_ref[0:POOL_HALO, :] = jnp.zeros((POOL_HALO, d), F32)

    h_ref[POOL_HALO:, :] = _rmsnorm(x, g_ref[...])
    t = i * tm + lax.broadcasted_iota(jnp.int32, (tm, 1), 0)
    for g, w in enumerate(POOL_WINDOWS):
        cols = slice(g * gd, (g + 1) * gd)
        win = h_ref[POOL_HALO:, cols]
        for back in range(1, w):
            win = win + h_ref[POOL_HALO - back:POOL_HALO - back + tm, cols]
        cnt = jnp.minimum(t + 1, w).astype(F32)
        y = (win / cnt - h_ref[POOL_HALO:, cols]).astype(BF16)
        mixed = jnp.dot(y, w_ref[g], preferred_element_type=F32)
        o_ref[:, cols] = x[:, cols] + mixed * sc_ref[:, cols]
    h_ref[0:POOL_HALO, :] = h_ref[tm:tm + POOL_HALO, :]


def _pool(x, gain, w, scale, *, tm=512):
    s, d = x.shape
    row = pl.BlockSpec((tm, d), lambda i: (i, 0))
    return pl.pallas_call(
        _pool_kernel,
        grid=(s // tm,),
        in_specs=[row, _resident((1, d)), _resident(w.shape), _resident((1, d))],
        out_specs=row,
        out_shape=jax.ShapeDtypeStruct((s, d), F32),
        scratch_shapes=[pltpu.VMEM((tm + POOL_HALO, d), F32)],
        compiler_params=_params(),
        name="pool_mixer",
    )(x, gain, w, scale)


def kernel(x, ln_gains, ffn_w_gate, ffn_w_up, ffn_w_down, attn_w_qkv, attn_w_o,
           pool_w, pool_scale, final_gain):
    b, s, d = x.shape
    assert b == 1
    depth = ln_gains.shape[0]
    slopes = LOG2E * jnp.exp2(-8.0 * jnp.arange(1, N_HEADS + 1, dtype=F32) / N_HEADS)
    fg = final_gain.reshape(1, d)
    y = x.reshape(s, d)
    for i in range(depth):
        gains = ln_gains[i].reshape(3, 1, d)

        def half_ffn(y, which, final_norm=False, i=i, gains=gains):
            return _ffn(y, gains[2 * which],
                        ffn_w_gate[i, which].astype(BF16),
                        ffn_w_up[i, which].astype(BF16),
                        ffn_w_down[i, which].astype(BF16),
                        fg, final_norm=final_norm)

        y = half_ffn(y, 0)
        m = i // 2
        if i % 2 == 0:
            w_qkv = attn_w_qkv[m]
            q, k, vt, kmean = _qkv(y, gains[1], w_qkv[:, :2 * d].astype(BF16),
                                   w_qkv[:, 2 * d:].T.astype(BF16))
            o = _attention(slopes, q, k, vt, kmean.reshape(-1, d))
            y = _oproj(y, o, attn_w_o[m].astype(BF16))
        else:
            y = _pool(y, gains[1], pool_w[m].astype(BF16),
                      pool_scale[m].reshape(1, d))
        y = half_ffn(y, 1, final_norm=(i == depth - 1))
    return y.reshape(b, s, d)
```

```python
import functools

import jax
import jax.numpy as jnp
from jax import lax
from jax.experimental import pallas as pl
from jax.experimental.pallas import tpu as pltpu

N_HEADS = 16
HEAD_DIM = 64
MOBA_BLOCK = 256
MOBA_TOPK = 3
POOL_WINDOWS = (2, 4, 8, 16)
EPS = 1e-6
NEG = -1e30
LOG2E = 1.4426950408889634
UNDERFLOW = 140.0
NORM_SLACK = 1.02
NORM_CHUNK = 2048

LANES = 128
HEADS_PER_STEP = LANES // HEAD_DIM
ONES_ROWS = 16
POOL_HALO = 16
VMEM_LIMIT = 56 * 1024 * 1024

F32 = jnp.float32
BF16 = jnp.bfloat16


def _params():
    return pltpu.CompilerParams(
        dimension_semantics=("arbitrary",), vmem_limit_bytes=VMEM_LIMIT)


def _resident(shape):
    return pl.BlockSpec(shape, lambda *_: (0,) * len(shape),
                        pipeline_mode=pl.Buffered(1))


def _rmsnorm(x, g):
    ms = jnp.mean(x * x, axis=-1, keepdims=True)
    return x * lax.rsqrt(ms + EPS) * g


def _ffn_kernel(x_ref, g_ref, wg_ref, wu_ref, wd_ref, fg_ref, o_ref, a_ref, *,
                ff_chunk, final_norm):
    x = x_ref[...]
    h = _rmsnorm(x, g_ref[...]).astype(BF16)
    d_ff = wg_ref.shape[1]
    for c in range(d_ff // ff_chunk):
        sl = slice(c * ff_chunk, (c + 1) * ff_chunk)
        gate = jnp.dot(h, wg_ref[:, sl], preferred_element_type=F32)
        up = jnp.dot(h, wu_ref[:, sl], preferred_element_type=F32)
        a_ref[:, sl] = (gate * jax.nn.sigmoid(gate) * up).astype(BF16)
    y = x + 0.5 * jnp.dot(a_ref[...], wd_ref[...], preferred_element_type=F32)
    if final_norm:
        y = _rmsnorm(y, fg_ref[...])
    o_ref[...] = y


def _ffn(x, gain, wg, wu, wd, final_gain, *, final_norm, tm=512, ff_chunk=256):
    s, d = x.shape
    d_ff = wg.shape[1]
    row = pl.BlockSpec((tm, d), lambda i: (i, 0))
    return pl.pallas_call(
        functools.partial(_ffn_kernel, ff_chunk=ff_chunk, final_norm=final_norm),
        grid=(s // tm,),
        in_specs=[row, _resident((1, d)), _resident((d, d_ff)),
                  _resident((d, d_ff)), _resident((d_ff, d)), _resident((1, d))],
        out_specs=row,
        out_shape=jax.ShapeDtypeStruct((s, d), F32),
        scratch_shapes=[pltpu.VMEM((tm, d_ff), BF16)],
        compiler_params=_params(),
        name="ffn",
    )(x, gain, wg, wu, wd, final_gain)


def _qkv_kernel(x_ref, g_ref, wqk_ref, wvt_ref, q_ref, k_ref, vt_ref, km_ref):
    d = x_ref.shape[1]
    h = _rmsnorm(x_ref[...], g_ref[...]).astype(BF16)
    qk = jnp.dot(h, wqk_ref[...], preferred_element_type=F32)
    q_ref[...] = (qk[:, :d] * (HEAD_DIM ** -0.5 * LOG2E)).astype(BF16)
    k = qk[:, d:]
    k_ref[...] = k.astype(BF16)
    vt = lax.dot_general(wvt_ref[...], h, (((1,), (1,)), ((), ())),
                         preferred_element_type=F32).astype(BF16)
    for b in range(x_ref.shape[0] // MOBA_BLOCK):
        rows = slice(b * MOBA_BLOCK, (b + 1) * MOBA_BLOCK)
        km_ref[b] = jnp.mean(k[rows], axis=0, keepdims=True)
        vt_ref[b] = vt[:, rows]


def _qkv(x, gain, wqk, wvt, *, tm=512):
    s, d = x.shape
    row = pl.BlockSpec((tm, d), lambda i: (i, 0))
    nb_tile = tm // MOBA_BLOCK
    nb = s // MOBA_BLOCK
    return pl.pallas_call(
        _qkv_kernel,
        grid=(s // tm,),
        in_specs=[row, _resident((1, d)), _resident((d, 2 * d)), _resident((d, d))],
        out_specs=[row, row,
                   pl.BlockSpec((nb_tile, d, MOBA_BLOCK), lambda i: (i, 0, 0)),
                   pl.BlockSpec((nb_tile, 1, d), lambda i: (i, 0, 0))],
        out_shape=[jax.ShapeDtypeStruct((s, d), BF16)] * 2
        + [jax.ShapeDtypeStruct((nb, d, MOBA_BLOCK), BF16),
           jax.ShapeDtypeStruct((nb, 1, d), F32)],
        compiler_params=_params(),
        name="qkv",
    )(x, gain, wqk, wvt)


def _attn_kernel(slopes_ref, q_ref, k_ref, vt_ref, km_ref, o_ref,
                 bias_ref, add_ref, kn_ref, s0_ref, s1_ref, p0_ref, p1_ref):
    pair = pl.program_id(0)
    jq = pl.program_id(1)
    bs = MOBA_BLOCK
    nb = km_ref.shape[0]
    nt = (((1,), (1,)), ((), ()))
    heads = range(HEADS_PER_STEP)
    halves = [slice(h * LANES, (h + 1) * LANES) for h in range(bs // LANES)]
    s_refs = (s0_ref, s1_ref)
    p_refs = (p0_ref, p1_ref)
    q = q_ref[...]
    lane = lax.broadcasted_iota(jnp.int32, (1, LANES), 1)
    rel = (lax.broadcasted_iota(jnp.int32, (bs, bs), 1)
           - lax.broadcasted_iota(jnp.int32, (bs, bs), 0))
    blk = lax.broadcasted_iota(jnp.int32, (nb, bs), 0)
    blkf = blk.astype(F32)
    past = blk < jq

    lane_head = (lax.broadcasted_iota(jnp.int32, (LANES, LANES), 0) // HEAD_DIM
                 == lax.broadcasted_iota(jnp.int32, (LANES, LANES), 1)).astype(BF16)

    def max_head_norm_sq(x):
        xf = x.astype(F32)
        sq = jnp.dot((xf * xf).astype(BF16), lane_head, preferred_element_type=F32)
        return jnp.max(sq, axis=0, keepdims=True)

    @pl.when(jq == 0)
    def _():
        for hh in heads:
            bias_ref[hh] = slopes_ref[pair * HEADS_PER_STEP + hh] * rel.astype(F32)
        kn = jnp.zeros((1, LANES), F32)
        for c in range(k_ref.shape[0] // NORM_CHUNK):
            kn = jnp.maximum(kn, max_head_norm_sq(k_ref[c * NORM_CHUNK:(c + 1) * NORM_CHUNK, :]))
        kn_ref[...] = kn

    qhs = []
    for hh in heads:
        in_head = (lane >= hh * HEAD_DIM) & (lane < (hh + 1) * HEAD_DIM)
        qhs.append(jnp.where(in_head, q, jnp.zeros_like(q)))

    def block_scores(n, hh):
        keys = k_ref[pl.ds(pl.multiple_of(n * bs, bs), bs), :]
        return lax.dot_general(keys, qhs[hh], nt, preferred_element_type=F32)

    ones = jnp.ones((ONES_ROWS, bs), BF16)

    def weighted_values(n, slot, hh):
        lhs = jnp.concatenate([vt_ref[n][hh * HEAD_DIM:(hh + 1) * HEAD_DIM, :], ones], axis=0)
        return jnp.dot(lhs, p_refs[slot][hh], preferred_element_type=F32)

    s_own = [block_scores(jq, hh) for hh in heads]

    km = km_ref[...]
    km_hi = km.astype(BF16)
    rest = km - km_hi.astype(F32)
    km_mid = rest.astype(BF16)
    km_lo = (rest - km_mid.astype(F32)).astype(BF16)
    km_terms = jnp.concatenate([km_hi, km_mid, km_lo], axis=0)
    for hh in heads:
        g3 = lax.dot_general(km_terms, qhs[hh], nt, preferred_element_type=F32)
        gate = g3[:nb] + g3[nb:2 * nb] + g3[2 * nb:]
        gate = jnp.where(past, gate, NEG)
        sel = jnp.zeros((nb, bs), jnp.bool_)
        for _ in range(MOBA_TOPK):
            top = jnp.max(gate, axis=0, keepdims=True)
            first = jnp.min(jnp.where(gate == top, blkf, float(nb)), axis=0, keepdims=True)
            pick = blkf == first
            sel = sel | (pick & past)
            gate = jnp.where(pick, -jnp.inf, gate)
        block_bias = (-slopes_ref[pair * HEADS_PER_STEP + hh] * bs) * (jq - blk).astype(F32)
        add_ref[hh] = jnp.where(sel, block_bias, NEG)

    causal = jnp.where(rel >= 0, 0.0, NEG)
    init = []
    for hh in heads:
        m = []
        for cols in halves:
            sb = s_own[hh][:, cols] - bias_ref[hh, :, cols] + causal[:, cols]
            m.append(jnp.max(sb, axis=0, keepdims=True))
            p_refs[1][hh, :, cols] = jnp.exp2(sb - m[-1]).astype(BF16)
        init.append((tuple(m), jnp.zeros((1, bs), F32), jnp.zeros((HEAD_DIM, bs), F32),
                     jnp.ones((1, bs), F32)))

    qk_bound = jnp.sqrt(max_head_norm_sq(q) * kn_ref[...] * NORM_SLACK)
    first_block = None
    for hh in heads:
        m_min = jnp.min(jnp.minimum(*init[hh][0]), axis=1, keepdims=True)
        reach = (UNDERFLOW + qk_bound[:, hh:hh + 1] - m_min) / slopes_ref[pair * HEADS_PER_STEP + hh]
        near = jq.astype(F32) - jnp.floor((reach - 1.0) / bs) - 1.0
        near = jnp.where(near > 0.0, near, 0.0)
        first_block = near if first_block is None else jnp.minimum(first_block, near)
    i_start = first_block[0, 0].astype(jnp.int32) // 2

    for hh in heads:
        s_refs[0][hh] = block_scores(2 * i_start, hh)

    def step(n, n_prev, cur, carry):
        prv = 1 - cur
        folded = []
        for hh in heads:
            _, l, acc, alpha_prev = carry[hh]
            r = weighted_values(n_prev, prv, hh)
            folded.append((alpha_prev * l + r[HEAD_DIM:HEAD_DIM + 1],
                           alpha_prev * acc + r[:HEAD_DIM]))
        n_next = jnp.minimum(n + 1, nb - 1)
        for hh in heads:
            s_refs[prv][hh] = block_scores(n_next, hh)
        out = []
        for hh in heads:
            m = carry[hh][0]
            row = add_ref[hh, pl.ds(n, 1), :]
            m_new, alpha = [], []
            for h, cols in enumerate(halves):
                sb = s_refs[cur][hh, :, cols] - bias_ref[hh, :, cols]
                top = jnp.max(sb, axis=0, keepdims=True) + row[:, cols]
                m_new.append(jnp.maximum(m[h], top))
                p_refs[cur][hh, :, cols] = jnp.exp2(sb - (m_new[h] - row[:, cols])).astype(BF16)
                alpha.append(jnp.exp2(m[h] - m_new[h]))
            out.append((tuple(m_new), *folded[hh], jnp.concatenate(alpha, axis=1)))
        return tuple(out)

    def body(i, carry):
        carry = step(2 * i, jnp.where(i == i_start, jq, 2 * i - 1), 0, carry)
        return step(2 * i + 1, 2 * i, 1, carry)

    i_stop = (jq + 1) // 2
    fin = lax.fori_loop(i_start, i_stop, body, tuple(init))
    n_last = jnp.where(i_stop == i_start, jq, 2 * i_stop - 1)
    o_t = []
    for hh in heads:
        _, l, acc, alpha = fin[hh]
        r = weighted_values(n_last, 1, hh)
        o_t.append((alpha * acc + r[:HEAD_DIM]) / (alpha * l + r[HEAD_DIM:HEAD_DIM + 1]))
    o_ref[...] = jnp.concatenate(o_t, axis=0).T.astype(BF16)


def _attention(slopes, q, k, vt, kmean):
    s, d = q.shape
    nb = kmean.shape[0]
    tile = pl.BlockSpec((MOBA_BLOCK, LANES), lambda p, j: (j, p))
    return pl.pallas_call(
        _attn_kernel,
        grid=(d // LANES, s // MOBA_BLOCK),
        in_specs=[pl.BlockSpec(memory_space=pltpu.SMEM), tile,
                  pl.BlockSpec((s, LANES), lambda p, j: (0, p)),
                  pl.BlockSpec((nb, LANES, MOBA_BLOCK), lambda p, j: (0, p, 0)),
                  pl.BlockSpec((nb, LANES), lambda p, j: (0, p))],
        out_specs=tile,
        out_shape=jax.ShapeDtypeStruct((s, d), BF16),
        scratch_shapes=[pltpu.VMEM((HEADS_PER_STEP, MOBA_BLOCK, MOBA_BLOCK), F32),
                        pltpu.VMEM((HEADS_PER_STEP, nb, MOBA_BLOCK), F32),
                        pltpu.VMEM((1, LANES), F32),
                        pltpu.VMEM((HEADS_PER_STEP, MOBA_BLOCK, MOBA_BLOCK), F32),
                        pltpu.VMEM((HEADS_PER_STEP, MOBA_BLOCK, MOBA_BLOCK), F32),
                        pltpu.VMEM((HEADS_PER_STEP, MOBA_BLOCK, MOBA_BLOCK), BF16),
                        pltpu.VMEM((HEADS_PER_STEP, MOBA_BLOCK, MOBA_BLOCK), BF16)],
        compiler_params=pltpu.CompilerParams(
            dimension_semantics=("arbitrary", "arbitrary"),
            vmem_limit_bytes=VMEM_LIMIT),
        name="moba_attention",
    )(slopes, q, k, vt, kmean)


def _oproj_kernel(x_ref, o_ref, w_ref, y_ref):
    y_ref[...] = x_ref[...] + jnp.dot(o_ref[...], w_ref[...],
                                      preferred_element_type=F32)


def _oproj(x, o, w, *, tm=512):
    s, d = x.shape
    row = pl.BlockSpec((tm, d), lambda i: (i, 0))
    return pl.pallas_call(
        _oproj_kernel,
        grid=(s // tm,),
        in_specs=[row, row, _resident((d, d))],
        out_specs=row,
        out_shape=jax.ShapeDtypeStruct((s, d), F32),
        compiler_params=_params(),
        name="attn_out_proj",
    )(x, o, w)


def _pool_kernel(x_ref, g_ref, w_ref, sc_ref, o_ref, h_ref):
    i = pl.program_id(0)
    tm, d = x_ref.shape
    gd = d // len(POOL_WINDOWS)
    x = x_ref[...]

    @pl.when(i == 0)
    def _():
        h_ref[0:POOL_HALO, :] = jnp.zeros((POOL_HALO, d), F32)

    h_ref[POOL_HALO:, :] = _rmsnorm(x, g_ref[...])
    t = i * tm + lax.broadcasted_iota(jnp.int32, (tm, 1), 0)
    for g, w in enumerate(POOL_WINDOWS):
        cols = slice(g * gd, (g + 1) * gd)
        win = h_ref[POOL_HALO:, cols]
        for back in range(1, w):
            win = win + h_ref[POOL_HALO - back:POOL_HALO - back + tm, cols]
        cnt = jnp.minimum(t + 1, w).astype(F32)
        y = (win / cnt - h_ref[POOL_HALO:, cols]).astype(BF16)
        mixed = jnp.dot(y, w_ref[g], preferred_element_type=F32)
        o_ref[:, cols] = x[:, cols] + mixed * sc_ref[:, cols]
    h_ref[0:POOL_HALO, :] = h_ref[tm:tm + POOL_HALO, :]


def _pool(x, gain, w, scale, *, tm=512):
    s, d = x.shape
    row = pl.BlockSpec((tm, d), lambda i: (i, 0))
    return pl.pallas_call(
        _pool_kernel,
        grid=(s // tm,),
        in_specs=[row, _resident((1, d)), _resident(w.shape), _resident((1, d))],
        out_specs=row,
        out_shape=jax.ShapeDtypeStruct((s, d), F32),
        scratch_shapes=[pltpu.VMEM((tm + POOL_HALO, d), F32)],
        compiler_params=_params(),
        name="pool_mixer",
    )(x, gain, w, scale)


def kernel(x, ln_gains, ffn_w_gate, ffn_w_up, ffn_w_down, attn_w_qkv, attn_w_o,
           pool_w, pool_scale, final_gain):
    b, s, d = x.shape
    assert b == 1
    depth = ln_gains.shape[0]
    slopes = LOG2E * jnp.exp2(-8.0 * jnp.arange(1, N_HEADS + 1, dtype=F32) / N_HEADS)
    fg = final_gain.reshape(1, d)
    y = x.reshape(s, d)
    for i in range(depth):
        gains = ln_gains[i].reshape(3, 1, d)

        def half_ffn(y, which, final_norm=False, i=i, gains=gains):
            return _ffn(y, gains[2 * which],
                        ffn_w_gate[i, which].astype(BF16),
                        ffn_w_up[i, which].astype(BF16),
                        ffn_w_down[i, which].astype(BF16),
                        fg, final_norm=final_norm)

        y = half_ffn(y, 0)
        m = i // 2
        if i % 2 == 0:
            w_qkv = attn_w_qkv[m]
            q, k, vt, kmean = _qkv(y, gains[1], w_qkv[:, :2 * d].astype(BF16),
                                   w_qkv[:, 2 * d:].T.astype(BF16))
            o = _attention(slopes, q, k, vt, kmean.reshape(-1, d))
            y = _oproj(y, o, attn_w_o[m].astype(BF16))
        else:
            y = _pool(y, gains[1], pool_w[m].astype(BF16),
                      pool_scale[m].reshape(1, d))
        y = half_ffn(y, 1, final_norm=(i == depth - 1))
    return y.reshape(b, s, d)
```

```python
import functools

import jax
import jax.numpy as jnp
from jax import lax
from jax.experimental import pallas as pl
from jax.experimental.pallas import tpu as pltpu

N_HEADS = 16
HEAD_DIM = 64
MOBA_BLOCK = 256
MOBA_TOPK = 3
POOL_WINDOWS = (2, 4, 8, 16)
EPS = 1e-6
NEG = -1e30
LOG2E = 1.4426950408889634
UNDERFLOW = 140.0
NORM_SLACK = 1.02
NORM_CHUNK = 2048
PAIRS_PER_ITER = 4

LANES = 128
HEADS_PER_STEP = LANES // HEAD_DIM
ONES_ROWS = 16
POOL_HALO = 16
VMEM_LIMIT = 56 * 1024 * 1024

F32 = jnp.float32
BF16 = jnp.bfloat16


def _params():
    return pltpu.CompilerParams(
        dimension_semantics=("arbitrary",), vmem_limit_bytes=VMEM_LIMIT)


def _resident(shape):
    return pl.BlockSpec(shape, lambda *_: (0,) * len(shape),
                        pipeline_mode=pl.Buffered(1))


def _rmsnorm(x, g):
    ms = jnp.mean(x * x, axis=-1, keepdims=True)
    return x * lax.rsqrt(ms + EPS) * g


def _ffn_kernel(x_ref, g_ref, wg_ref, wu_ref, wd_ref, fg_ref, o_ref, a_ref, *,
                ff_chunk, final_norm):
    x = x_ref[...]
    h = _rmsnorm(x, g_ref[...]).astype(BF16)
    d_ff = wg_ref.shape[1]
    for c in range(d_ff // ff_chunk):
        sl = slice(c * ff_chunk, (c + 1) * ff_chunk)
        gate = jnp.dot(h, wg_ref[:, sl], preferred_element_type=F32)
        up = jnp.dot(h, wu_ref[:, sl], preferred_element_type=F32)
        a_ref[:, sl] = (gate * jax.nn.sigmoid(gate) * up).astype(BF16)
    y = x + 0.5 * jnp.dot(a_ref[...], wd_ref[...], preferred_element_type=F32)
    if final_norm:
        y = _rmsnorm(y, fg_ref[...])
    o_ref[...] = y


def _ffn(x, gain, wg, wu, wd, final_gain, *, final_norm, tm=512, ff_chunk=256):
    s, d = x.shape
    d_ff = wg.shape[1]
    row = pl.BlockSpec((tm, d), lambda i: (i, 0))
    return pl.pallas_call(
        functools.partial(_ffn_kernel, ff_chunk=ff_chunk, final_norm=final_norm),
        grid=(s // tm,),
        in_specs=[row, _resident((1, d)), _resident((d, d_ff)),
                  _resident((d, d_ff)), _resident((d_ff, d)), _resident((1, d))],
        out_specs=row,
        out_shape=jax.ShapeDtypeStruct((s, d), F32),
        scratch_shapes=[pltpu.VMEM((tm, d_ff), BF16)],
        compiler_params=_params(),
        name="ffn",
    )(x, gain, wg, wu, wd, final_gain)


def _qkv_kernel(x_ref, g_ref, wqk_ref, wvt_ref, q_ref, k_ref, vt_ref, km_ref):
    d = x_ref.shape[1]
    h = _rmsnorm(x_ref[...], g_ref[...]).astype(BF16)
    qk = jnp.dot(h, wqk_ref[...], preferred_element_type=F32)
    q_ref[...] = (qk[:, :d] * (HEAD_DIM ** -0.5 * LOG2E)).astype(BF16)
    k = qk[:, d:]
    k_ref[...] = k.astype(BF16)
    vt = lax.dot_general(wvt_ref[...], h, (((1,), (1,)), ((), ())),
                         preferred_element_type=F32).astype(BF16)
    for b in range(x_ref.shape[0] // MOBA_BLOCK):
        rows = slice(b * MOBA_BLOCK, (b + 1) * MOBA_BLOCK)
        km_ref[b] = jnp.mean(k[rows], axis=0, keepdims=True)
        vt_ref[b] = vt[:, rows]


def _qkv(x, gain, wqk, wvt, *, tm=512):
    s, d = x.shape
    row = pl.BlockSpec((tm, d), lambda i: (i, 0))
    nb_tile = tm // MOBA_BLOCK
    nb = s // MOBA_BLOCK
    return pl.pallas_call(
        _qkv_kernel,
        grid=(s // tm,),
        in_specs=[row, _resident((1, d)), _resident((d, 2 * d)), _resident((d, d))],
        out_specs=[row, row,
                   pl.BlockSpec((nb_tile, d, MOBA_BLOCK), lambda i: (i, 0, 0)),
                   pl.BlockSpec((nb_tile, 1, d), lambda i: (i, 0, 0))],
        out_shape=[jax.ShapeDtypeStruct((s, d), BF16)] * 2
        + [jax.ShapeDtypeStruct((nb, d, MOBA_BLOCK), BF16),
           jax.ShapeDtypeStruct((nb, 1, d), F32)],
        compiler_params=_params(),
        name="qkv",
    )(x, gain, wqk, wvt)


def _attn_kernel(slopes_ref, q_ref, k_ref, vt_ref, km_ref, o_ref,
                 bias_ref, add_ref, kn_ref, s0_ref, s1_ref, p0_ref, p1_ref):
    pair = pl.program_id(0)
    jq = pl.program_id(1)
    bs = MOBA_BLOCK
    nb = km_ref.shape[0]
    nt = (((1,), (1,)), ((), ()))
    heads = range(HEADS_PER_STEP)
    halves = [slice(h * LANES, (h + 1) * LANES) for h in range(bs // LANES)]
    s_refs = (s0_ref, s1_ref)
    p_refs = (p0_ref, p1_ref)
    q = q_ref[...]
    lane = lax.broadcasted_iota(jnp.int32, (1, LANES), 1)
    rel = (lax.broadcasted_iota(jnp.int32, (bs, bs), 1)
           - lax.broadcasted_iota(jnp.int32, (bs, bs), 0))
    blk = lax.broadcasted_iota(jnp.int32, (nb, bs), 0)
    blkf = blk.astype(F32)
    past = blk < jq

    lane_head = (lax.broadcasted_iota(jnp.int32, (LANES, LANES), 0) // HEAD_DIM
                 == lax.broadcasted_iota(jnp.int32, (LANES, LANES), 1)).astype(BF16)

    def max_head_norm_sq(x):
        xf = x.astype(F32)
        sq = jnp.dot((xf * xf).astype(BF16), lane_head, preferred_element_type=F32)
        return jnp.max(sq, axis=0, keepdims=True)

    @pl.when(jq == 0)
    def _():
        for hh in heads:
            bias_ref[hh] = slopes_ref[pair * HEADS_PER_STEP + hh] * rel.astype(F32)
        kn = jnp.zeros((1, LANES), F32)
        for c in range(k_ref.shape[0] // NORM_CHUNK):
            kn = jnp.maximum(kn, max_head_norm_sq(k_ref[c * NORM_CHUNK:(c + 1) * NORM_CHUNK, :]))
        kn_ref[...] = kn

    qhs = []
    for hh in heads:
        in_head = (lane >= hh * HEAD_DIM) & (lane < (hh + 1) * HEAD_DIM)
        qhs.append(jnp.where(in_head, q, jnp.zeros_like(q)))

    def block_scores(n, hh):
        keys = k_ref[pl.ds(pl.multiple_of(n * bs, bs), bs), :]
        return lax.dot_general(keys, qhs[hh], nt, preferred_element_type=F32)

    ones = jnp.ones((ONES_ROWS, bs), BF16)

    def weighted_values(n, slot, hh):
        lhs = jnp.concatenate([vt_ref[n][hh * HEAD_DIM:(hh + 1) * HEAD_DIM, :], ones], axis=0)
        return jnp.dot(lhs, p_refs[slot][hh], preferred_element_type=F32)

    s_own = [block_scores(jq, hh) for hh in heads]

    km = km_ref[...]
    km_hi = km.astype(BF16)
    rest = km - km_hi.astype(F32)
    km_mid = rest.astype(BF16)
    km_lo = (rest - km_mid.astype(F32)).astype(BF16)
    km_terms = jnp.concatenate([km_hi, km_mid, km_lo], axis=0)
    for hh in heads:
        g3 = lax.dot_general(km_terms, qhs[hh], nt, preferred_element_type=F32)
        gate = g3[:nb] + g3[nb:2 * nb] + g3[2 * nb:]
        gate = jnp.where(past, gate, NEG)
        sel = jnp.zeros((nb, bs), jnp.bool_)
        for _ in range(MOBA_TOPK):
            top = jnp.max(gate, axis=0, keepdims=True)
            first = jnp.min(jnp.where(gate == top, blkf, float(nb)), axis=0, keepdims=True)
            pick = blkf == first
            sel = sel | (pick & past)
            gate = jnp.where(pick, -jnp.inf, gate)
        block_bias = (-slopes_ref[pair * HEADS_PER_STEP + hh] * bs) * (jq - blk).astype(F32)
        add_ref[hh] = jnp.where(sel, block_bias, NEG)

    causal = jnp.where(rel >= 0, 0.0, NEG)
    init = []
    for hh in heads:
        m = []
        for cols in halves:
            sb = s_own[hh][:, cols] - bias_ref[hh, :, cols] + causal[:, cols]
            m.append(jnp.max(sb, axis=0, keepdims=True))
            p_refs[1][hh, :, cols] = jnp.exp2(sb - m[-1]).astype(BF16)
        init.append((tuple(m), jnp.zeros((1, bs), F32), jnp.zeros((HEAD_DIM, bs), F32),
                     jnp.ones((1, bs), F32)))

    qk_bound = jnp.sqrt(max_head_norm_sq(q) * kn_ref[...] * NORM_SLACK)
    first_block = None
    for hh in heads:
        m_min = jnp.min(jnp.minimum(*init[hh][0]), axis=1, keepdims=True)
        reach = (UNDERFLOW + qk_bound[:, hh:hh + 1] - m_min) / slopes_ref[pair * HEADS_PER_STEP + hh]
        near = jq.astype(F32) - jnp.floor((reach - 1.0) / bs) - 1.0
        near = jnp.where(near > 0.0, near, 0.0)
        first_block = near if first_block is None else jnp.minimum(first_block, near)
    i_start = first_block[0, 0].astype(jnp.int32) // 2

    for hh in heads:
        s_refs[0][hh] = block_scores(2 * i_start, hh)

    def step(n, n_prev, cur, carry):
        prv = 1 - cur
        folded = []
        for hh in heads:
            _, l, acc, alpha_prev = carry[hh]
            r = weighted_values(n_prev, prv, hh)
            folded.append((alpha_prev * l + r[HEAD_DIM:HEAD_DIM + 1],
                           alpha_prev * acc + r[:HEAD_DIM]))
        n_next = jnp.minimum(n + 1, nb - 1)
        for hh in heads:
            s_refs[prv][hh] = block_scores(n_next, hh)
        out = []
        for hh in heads:
            m = carry[hh][0]
            row = add_ref[hh, pl.ds(n, 1), :]
            m_new, alpha = [], []
            for h, cols in enumerate(halves):
                sb = s_refs[cur][hh, :, cols] - bias_ref[hh, :, cols]
                top = jnp.max(sb, axis=0, keepdims=True) + row[:, cols]
                m_new.append(jnp.maximum(m[h], top))
                p_refs[cur][hh, :, cols] = jnp.exp2(sb - (m_new[h] - row[:, cols])).astype(BF16)
                alpha.append(jnp.exp2(m[h] - m_new[h]))
            out.append((tuple(m_new), *folded[hh], jnp.concatenate(alpha, axis=1)))
        return tuple(out)

    def body(i, carry):
        carry = step(2 * i, jnp.where(i == i_start, jq, 2 * i - 1), 0, carry)
        return step(2 * i + 1, 2 * i, 1, carry)

    i_stop = (jq + 1) // 2
    n_long = (i_stop - i_start) // PAIRS_PER_ITER

    def long_body(j, carry):
        for u in range(PAIRS_PER_ITER):
            carry = body(i_start + PAIRS_PER_ITER * j + u, carry)
        return carry

    mid = lax.fori_loop(0, n_long, long_body, tuple(init))
    fin = lax.fori_loop(i_start + PAIRS_PER_ITER * n_long, i_stop, body, mid)
    n_last = jnp.where(i_stop == i_start, jq, 2 * i_stop - 1)
    o_t = []
    for hh in heads:
        _, l, acc, alpha = fin[hh]
        r = weighted_values(n_last, 1, hh)
        o_t.append((alpha * acc + r[:HEAD_DIM]) / (alpha * l + r[HEAD_DIM:HEAD_DIM + 1]))
    o_ref[...] = jnp.concatenate(o_t, axis=0).T.astype(BF16)


def _attention(slopes, q, k, vt, kmean):
    s, d = q.shape
    nb = kmean.shape[0]
    tile = pl.BlockSpec((MOBA_BLOCK, LANES), lambda p, j: (j, p))
    return pl.pallas_call(
        _attn_kernel,
        grid=(d // LANES, s // MOBA_BLOCK),
        in_specs=[pl.BlockSpec(memory_space=pltpu.SMEM), tile,
                  pl.BlockSpec((s, LANES), lambda p, j: (0, p)),
                  pl.BlockSpec((nb, LANES, MOBA_BLOCK), lambda p, j: (0, p, 0)),
                  pl.BlockSpec((nb, LANES), lambda p, j: (0, p))],
        out_specs=tile,
        out_shape=jax.ShapeDtypeStruct((s, d), BF16),
        scratch_shapes=[pltpu.VMEM((HEADS_PER_STEP, MOBA_BLOCK, MOBA_BLOCK), F32),
                        pltpu.VMEM((HEADS_PER_STEP, nb, MOBA_BLOCK), F32),
                        pltpu.VMEM((1, LANES), F32),
                        pltpu.VMEM((HEADS_PER_STEP, MOBA_BLOCK, MOBA_BLOCK), F32),
                        pltpu.VMEM((HEADS_PER_STEP, MOBA_BLOCK, MOBA_BLOCK), F32),
                        pltpu.VMEM((HEADS_PER_STEP, MOBA_BLOCK, MOBA_BLOCK), BF16),
                        pltpu.VMEM((HEADS_PER_STEP, MOBA_BLOCK, MOBA_BLOCK), BF16)],
        compiler_params=pltpu.CompilerParams(
            dimension_semantics=("arbitrary", "arbitrary"),
            vmem_limit_bytes=VMEM_LIMIT),
        name="moba_attention",
    )(slopes, q, k, vt, kmean)


def _oproj_kernel(x_ref, o_ref, w_ref, y_ref):
    y_ref[...] = x_ref[...] + jnp.dot(o_ref[...], w_ref[...],
                                      preferred_element_type=F32)


def _oproj(x, o, w, *, tm=512):
    s, d = x.shape
    row = pl.BlockSpec((tm, d), lambda i: (i, 0))
    return pl.pallas_call(
        _oproj_kernel,
        grid=(s // tm,),
        in_specs=[row, row, _resident((d, d))],
        out_specs=row,
        out_shape=jax.ShapeDtypeStruct((s, d), F32),
        compiler_params=_params(),
        name="attn_out_proj",
    )(x, o, w)


def _pool_kernel(x_ref, g_ref, w_ref, sc_ref, o_ref, h_ref):
    i = pl.program_id(0)
    tm, d = x_ref.shape
    gd = d // len(POOL_WINDOWS)
    x = x_ref[...]

    @pl.when(i == 0)
    def _():
        h_ref[0:POOL_HALO, :] = jnp.zeros((POOL_HALO, d), F32)

    h_ref[POOL_HALO:, :] = _rmsnorm(x, g_ref[...])
    t = i * tm + lax.broadcasted_iota(jnp.int32, (tm, 1), 0)
    for g, w in enumerate(POOL_WINDOWS):
        cols = slice(g * gd, (g + 1) * gd)
        win = h_ref[POOL_HALO:, cols]
        for back in range(1, w):
            win = win + h_ref[POOL_HALO - back:POOL_HALO - back + tm, cols]
        cnt = jnp.minimum(t + 1, w).astype(F32)
        y = (win / cnt - h_ref[POOL_HALO:, cols]).astype(BF16)
        mixed = jnp.dot(y, w_ref[g], preferred_element_type=F32)
        o_ref[:, cols] = x[:, cols] + mixed * sc_ref[:, cols]
    h_ref[0:POOL_HALO, :] = h_ref[tm:tm + POOL_HALO, :]


def _pool(x, gain, w, scale, *, tm=512):
    s, d = x.shape
    row = pl.BlockSpec((tm, d), lambda i: (i, 0))
    return pl.pallas_call(
        _pool_kernel,
        grid=(s // tm,),
        in_specs=[row, _resident((1, d)), _resident(w.shape), _resident((1, d))],
        out_specs=row,
        out_shape=jax.ShapeDtypeStruct((s, d), F32),
        scratch_shapes=[pltpu.VMEM((tm + POOL_HALO, d), F32)],
        compiler_params=_params(),
        name="pool_mixer",
    )(x, gain, w, scale)


def kernel(x, ln_gains, ffn_w_gate, ffn_w_up, ffn_w_down, attn_w_qkv, attn_w_o,
           pool_w, pool_scale, final_gain):
    b, s, d = x.shape
    assert b == 1
    depth = ln_gains.shape[0]
    slopes = LOG2E * jnp.exp2(-8.0 * jnp.arange(1, N_HEADS + 1, dtype=F32) / N_HEADS)
    fg = final_gain.reshape(1, d)
    y = x.reshape(s, d)
    for i in range(depth):
        gains = ln_gains[i].reshape(3, 1, d)

        def half_ffn(y, which, final_norm=False, i=i, gains=gains):
            return _ffn(y, gains[2 * which],
                        ffn_w_gate[i, which].astype(BF16),
                        ffn_w_up[i, which].astype(BF16),
                        ffn_w_down[i, which].astype(BF16),
                        fg, final_norm=final_norm)

        y = half_ffn(y, 0)
        m = i // 2
        if i % 2 == 0:
            w_qkv = attn_w_qkv[m]
            q, k, vt, kmean = _qkv(y, gains[1], w_qkv[:, :2 * d].astype(BF16),
                                   w_qkv[:, 2 * d:].T.astype(BF16))
            o = _attention(slopes, q, k, vt, kmean.reshape(-1, d))
            y = _oproj(y, o, attn_w_o[m].astype(BF16))
        else:
            y = _pool(y, gains[1], pool_w[m].astype(BF16),
                      pool_scale[m].reshape(1, d))
        y = half_ffn(y, 1, final_norm=(i == depth - 1))
    return y.reshape(b, s, d)
```

```python
import functools

import jax
import jax.numpy as jnp
from jax import lax
from jax.experimental import pallas as pl
from jax.experimental.pallas import tpu as pltpu

N_HEADS = 16
HEAD_DIM = 64
MOBA_BLOCK = 256
MOBA_TOPK = 3
POOL_WINDOWS = (2, 4, 8, 16)
EPS = 1e-6
NEG = -1e30
LOG2E = 1.4426950408889634
UNDERFLOW = 140.0
NORM_SLACK = 1.02
NORM_CHUNK = 2048
PAIRS_PER_ITER = 8

LANES = 128
HEADS_PER_STEP = LANES // HEAD_DIM
ONES_ROWS = 16
POOL_HALO = 16
VMEM_LIMIT = 56 * 1024 * 1024

F32 = jnp.float32
BF16 = jnp.bfloat16


def _params():
    return pltpu.CompilerParams(
        dimension_semantics=("arbitrary",), vmem_limit_bytes=VMEM_LIMIT)


def _resident(shape):
    return pl.BlockSpec(shape, lambda *_: (0,) * len(shape),
                        pipeline_mode=pl.Buffered(1))


def _rmsnorm(x, g):
    ms = jnp.mean(x * x, axis=-1, keepdims=True)
    return x * lax.rsqrt(ms + EPS) * g


def _ffn_kernel(x_ref, g_ref, wg_ref, wu_ref, wd_ref, fg_ref, o_ref, a_ref, *,
                ff_chunk, final_norm):
    x = x_ref[...]
    h = _rmsnorm(x, g_ref[...]).astype(BF16)
    d_ff = wg_ref.shape[1]
    for c in range(d_ff // ff_chunk):
        sl = slice(c * ff_chunk, (c + 1) * ff_chunk)
        gate = jnp.dot(h, wg_ref[:, sl], preferred_element_type=F32)
        up = jnp.dot(h, wu_ref[:, sl], preferred_element_type=F32)
        a_ref[:, sl] = (gate * jax.nn.sigmoid(gate) * up).astype(BF16)
    y = x + 0.5 * jnp.dot(a_ref[...], wd_ref[...], preferred_element_type=F32)
    if final_norm:
        y = _rmsnorm(y, fg_ref[...])
    o_ref[...] = y


def _ffn(x, gain, wg, wu, wd, final_gain, *, final_norm, tm=512, ff_chunk=256):
    s, d = x.shape
    d_ff = wg.shape[1]
    row = pl.BlockSpec((tm, d), lambda i: (i, 0))
    return pl.pallas_call(
        functools.partial(_ffn_kernel, ff_chunk=ff_chunk, final_norm=final_norm),
        grid=(s // tm,),
        in_specs=[row, _resident((1, d)), _resident((d, d_ff)),
                  _resident((d, d_ff)), _resident((d_ff, d)), _resident((1, d))],
        out_specs=row,
        out_shape=jax.ShapeDtypeStruct((s, d), F32),
        scratch_shapes=[pltpu.VMEM((tm, d_ff), BF16)],
        compiler_params=_params(),
        name="ffn",
    )(x, gain, wg, wu, wd, final_gain)


def _qkv_kernel(x_ref, g_ref, wqk_ref, wvt_ref, q_ref, k_ref, vt_ref, km_ref):
    d = x_ref.shape[1]
    h = _rmsnorm(x_ref[...], g_ref[...]).astype(BF16)
    qk = jnp.dot(h, wqk_ref[...], preferred_element_type=F32)
    q_ref[...] = (qk[:, :d] * (HEAD_DIM ** -0.5 * LOG2E)).astype(BF16)
    k = qk[:, d:]
    k_ref[...] = k.astype(BF16)
    vt = lax.dot_general(wvt_ref[...], h, (((1,), (1,)), ((), ())),
                         preferred_element_type=F32).astype(BF16)
    for b in range(x_ref.shape[0] // MOBA_BLOCK):
        rows = slice(b * MOBA_BLOCK, (b + 1) * MOBA_BLOCK)
        km_ref[b] = jnp.mean(k[rows], axis=0, keepdims=True)
        vt_ref[b] = vt[:, rows]


def _qkv(x, gain, wqk, wvt, *, tm=512):
    s, d = x.shape
    row = pl.BlockSpec((tm, d), lambda i: (i, 0))
    nb_tile = tm // MOBA_BLOCK
    nb = s // MOBA_BLOCK
    return pl.pallas_call(
        _qkv_kernel,
        grid=(s // tm,),
        in_specs=[row, _resident((1, d)), _resident((d, 2 * d)), _resident((d, d))],
        out_specs=[row, row,
                   pl.BlockSpec((nb_tile, d, MOBA_BLOCK), lambda i: (i, 0, 0)),
                   pl.BlockSpec((nb_tile, 1, d), lambda i: (i, 0, 0))],
        out_shape=[jax.ShapeDtypeStruct((s, d), BF16)] * 2
        + [jax.ShapeDtypeStruct((nb, d, MOBA_BLOCK), BF16),
           jax.ShapeDtypeStruct((nb, 1, d), F32)],
        compiler_params=_params(),
        name="qkv",
    )(x, gain, wqk, wvt)


def _attn_kernel(slopes_ref, q_ref, k_ref, vt_ref, km_ref, o_ref,
                 bias_ref, add_ref, kn_ref, s0_ref, s1_ref, p0_ref, p1_ref):
    pair = pl.program_id(0)
    jq = pl.program_id(1)
    bs = MOBA_BLOCK
    nb = km_ref.shape[0]
    nt = (((1,), (1,)), ((), ()))
    heads = range(HEADS_PER_STEP)
    halves = [slice(h * LANES, (h + 1) * LANES) for h in range(bs // LANES)]
    s_refs = (s0_ref, s1_ref)
    p_refs = (p0_ref, p1_ref)
    q = q_ref[...]
    lane = lax.broadcasted_iota(jnp.int32, (1, LANES), 1)
    rel = (lax.broadcasted_iota(jnp.int32, (bs, bs), 1)
           - lax.broadcasted_iota(jnp.int32, (bs, bs), 0))
    blk = lax.broadcasted_iota(jnp.int32, (nb, bs), 0)
    blkf = blk.astype(F32)
    past = blk < jq

    lane_head = (lax.broadcasted_iota(jnp.int32, (LANES, LANES), 0) // HEAD_DIM
                 == lax.broadcasted_iota(jnp.int32, (LANES, LANES), 1)).astype(BF16)

    def max_head_norm_sq(x):
        xf = x.astype(F32)
        sq = jnp.dot((xf * xf).astype(BF16), lane_head, preferred_element_type=F32)
        return jnp.max(sq, axis=0, keepdims=True)

    @pl.when(jq == 0)
    def _():
        for hh in heads:
            bias_ref[hh] = slopes_ref[pair * HEADS_PER_STEP + hh] * rel.astype(F32)
        kn = jnp.zeros((1, LANES), F32)
        for c in range(k_ref.shape[0] // NORM_CHUNK):
            kn = jnp.maximum(kn, max_head_norm_sq(k_ref[c * NORM_CHUNK:(c + 1) * NORM_CHUNK, :]))
        kn_ref[...] = kn

    qhs = []
    for hh in heads:
        in_head = (lane >= hh * HEAD_DIM) & (lane < (hh + 1) * HEAD_DIM)
        qhs.append(jnp.where(in_head, q, jnp.zeros_like(q)))

    def block_scores(n, hh):
        keys = k_ref[pl.ds(pl.multiple_of(n * bs, bs), bs), :]
        return lax.dot_general(keys, qhs[hh], nt, preferred_element_type=F32)

    ones = jnp.ones((ONES_ROWS, bs), BF16)

    def weighted_values(n, slot, hh):
        lhs = jnp.concatenate([vt_ref[n][hh * HEAD_DIM:(hh + 1) * HEAD_DIM, :], ones], axis=0)
        return jnp.dot(lhs, p_refs[slot][hh], preferred_element_type=F32)

    i_stop = (jq + 1) // 2
    n_top = 2 * i_stop - 1
    s_own = [block_scores(jq, hh) for hh in heads]
    for hh in heads:
        s_refs[0][hh] = block_scores(jnp.maximum(n_top, 0), hh)

    km = km_ref[...]
    km_hi = km.astype(BF16)
    rest = km - km_hi.astype(F32)
    km_mid = rest.astype(BF16)
    km_lo = (rest - km_mid.astype(F32)).astype(BF16)
    km_terms = jnp.concatenate([km_hi, km_mid, km_lo], axis=0)
    for hh in heads:
        g3 = lax.dot_general(km_terms, qhs[hh], nt, preferred_element_type=F32)
        gate = g3[:nb] + g3[nb:2 * nb] + g3[2 * nb:]
        gate = jnp.where(past, gate, NEG)
        sel = jnp.zeros((nb, bs), jnp.bool_)
        for _ in range(MOBA_TOPK):
            top = jnp.max(gate, axis=0, keepdims=True)
            first = jnp.min(jnp.where(gate == top, blkf, float(nb)), axis=0, keepdims=True)
            pick = blkf == first
            sel = sel | (pick & past)
            gate = jnp.where(pick, -jnp.inf, gate)
        block_bias = (-slopes_ref[pair * HEADS_PER_STEP + hh] * bs) * (jq - blk).astype(F32)
        add_ref[hh] = jnp.where(sel, block_bias, NEG)

    causal = jnp.where(rel >= 0, 0.0, NEG)
    init = []
    for hh in heads:
        m = []
        for cols in halves:
            sb = s_own[hh][:, cols] - bias_ref[hh, :, cols] + causal[:, cols]
            m.append(jnp.max(sb, axis=0, keepdims=True))
            p_refs[1][hh, :, cols] = jnp.exp2(sb - m[-1]).astype(BF16)
        init.append((tuple(m), jnp.zeros((1, bs), F32), jnp.zeros((HEAD_DIM, bs), F32),
                     jnp.ones((1, bs), F32)))

    qk_bound = jnp.sqrt(max_head_norm_sq(q) * kn_ref[...] * NORM_SLACK)
    first_block = None
    for hh in heads:
        m_min = jnp.min(jnp.minimum(*init[hh][0]), axis=1, keepdims=True)
        reach = (UNDERFLOW + qk_bound[:, hh:hh + 1] - m_min) / slopes_ref[pair * HEADS_PER_STEP + hh]
        near = jq.astype(F32) - jnp.floor((reach - 1.0) / bs) - 1.0
        near = jnp.where(near > 0.0, near, 0.0)
        first_block = near if first_block is None else jnp.minimum(first_block, near)
    i_start = first_block[0, 0].astype(jnp.int32) // 2

    def step(n, n_prev, cur, carry):
        folded = []
        for hh in heads:
            _, l, acc, alpha_prev = carry[hh]
            r = weighted_values(n_prev, 1 - cur, hh)
            folded.append((alpha_prev * l + r[HEAD_DIM:HEAD_DIM + 1],
                           alpha_prev * acc + r[:HEAD_DIM]))
        n_next = jnp.maximum(n - 1, 0)
        for hh in heads:
            s_refs[1 - cur][hh] = block_scores(n_next, hh)
        out = []
        for hh in heads:
            m = carry[hh][0]
            row = add_ref[hh, pl.ds(n, 1), :]
            m_new, alpha = [], []
            for h, cols in enumerate(halves):
                sb = s_refs[cur][hh, :, cols] - bias_ref[hh, :, cols]
                top = jnp.max(sb, axis=0, keepdims=True) + row[:, cols]
                m_new.append(jnp.maximum(m[h], top))
                p_refs[cur][hh, :, cols] = jnp.exp2(sb - (m_new[h] - row[:, cols])).astype(BF16)
                alpha.append(jnp.exp2(m[h] - m_new[h]))
            out.append((tuple(m_new), *folded[hh], jnp.concatenate(alpha, axis=1)))
        return tuple(out)

    def pairs(t0, count, carry):
        for t in range(count):
            n = n_top - 2 * (t0 + t)
            carry = step(n, jnp.where(n == n_top, jq, n + 1), 0, carry)
            carry = step(n - 1, n, 1, carry)
        return carry

    n_pairs = i_stop - i_start
    n_long = n_pairs // PAIRS_PER_ITER
    fin = lax.fori_loop(0, n_long,
                        lambda j, c: pairs(PAIRS_PER_ITER * j, PAIRS_PER_ITER, c), tuple(init))
    done = PAIRS_PER_ITER * n_long
    size = PAIRS_PER_ITER // 2
    while size:
        take = ((n_pairs - done) // size) % 2
        fin = lax.fori_loop(0, take, lambda _, c, done=done, size=size: pairs(done, size, c), fin)
        done = done + take * size
        size //= 2
    n_last = jnp.where(n_pairs == 0, jq, 2 * i_start)
    o_t = []
    for hh in heads:
        _, l, acc, alpha = fin[hh]
        r = weighted_values(n_last, 1, hh)
        o_t.append((alpha * acc + r[:HEAD_DIM]) / (alpha * l + r[HEAD_DIM:HEAD_DIM + 1]))
    o_ref[...] = jnp.concatenate(o_t, axis=0).T.astype(BF16)


def _attention(slopes, q, k, vt, kmean):
    s, d = q.shape
    nb = kmean.shape[0]
    tile = pl.BlockSpec((MOBA_BLOCK, LANES), lambda p, j: (j, p))
    return pl.pallas_call(
        _attn_kernel,
        grid=(d // LANES, s // MOBA_BLOCK),
        in_specs=[pl.BlockSpec(memory_space=pltpu.SMEM), tile,
                  pl.BlockSpec((s, LANES), lambda p, j: (0, p)),
                  pl.BlockSpec((nb, LANES, MOBA_BLOCK), lambda p, j: (0, p, 0)),
                  pl.BlockSpec((nb, LANES), lambda p, j: (0, p))],
        out_specs=tile,
        out_shape=jax.ShapeDtypeStruct((s, d), BF16),
        scratch_shapes=[pltpu.VMEM((HEADS_PER_STEP, MOBA_BLOCK, MOBA_BLOCK), F32),
                        pltpu.VMEM((HEADS_PER_STEP, nb, MOBA_BLOCK), F32),
                        pltpu.VMEM((1, LANES), F32),
                        pltpu.VMEM((HEADS_PER_STEP, MOBA_BLOCK, MOBA_BLOCK), F32),
                        pltpu.VMEM((HEADS_PER_STEP, MOBA_BLOCK, MOBA_BLOCK), F32),
                        pltpu.VMEM((HEADS_PER_STEP, MOBA_BLOCK, MOBA_BLOCK), BF16),
                        pltpu.VMEM((HEADS_PER_STEP, MOBA_BLOCK, MOBA_BLOCK), BF16)],
        compiler_params=pltpu.CompilerParams(
            dimension_semantics=("arbitrary", "arbitrary"),
            vmem_limit_bytes=VMEM_LIMIT),
        name="moba_attention",
    )(slopes, q, k, vt, kmean)


def _oproj_kernel(x_ref, o_ref, w_ref, y_ref):
    y_ref[...] = x_ref[...] + jnp.dot(o_ref[...], w_ref[...],
                                      preferred_element_type=F32)


def _oproj(x, o, w, *, tm=512):
    s, d = x.shape
    row = pl.BlockSpec((tm, d), lambda i: (i, 0))
    return pl.pallas_call(
        _oproj_kernel,
        grid=(s // tm,),
        in_specs=[row, row, _resident((d, d))],
        out_specs=row,
        out_shape=jax.ShapeDtypeStruct((s, d), F32),
        compiler_params=_params(),
        name="attn_out_proj",
    )(x, o, w)


def _pool_kernel(x_ref, g_ref, w_ref, sc_ref, o_ref, h_ref):
    i = pl.program_id(0)
    tm, d = x_ref.shape
    gd = d // len(POOL_WINDOWS)
    x = x_ref[...]

    @pl.when(i == 0)
    def _():
        h_ref[0:POOL_HALO, :] = jnp.zeros((POOL_HALO, d), F32)

    h_ref[POOL_HALO:, :] = _rmsnorm(x, g_ref[...])
    t = i * tm + lax.broadcasted_iota(jnp.int32, (tm, 1), 0)
    for g, w in enumerate(POOL_WINDOWS):
        cols = slice(g * gd, (g + 1) * gd)
        win = h_ref[POOL_HALO:, cols]
        for back in range(1, w):
            win = win + h_ref[POOL_HALO - back:POOL_HALO - back + tm, cols]
        cnt = jnp.minimum(t + 1, w).astype(F32)
        y = (win / cnt - h_ref[POOL_HALO:, cols]).astype(BF16)
        mixed = jnp.dot(y, w_ref[g], preferred_element_type=F32)
        o_ref[:, cols] = x[:, cols] + mixed * sc_ref[:, cols]
    h_ref[0:POOL_HALO, :] = h_ref[tm:tm + POOL_HALO, :]


def _pool(x, gain, w, scale, *, tm=512):
    s, d = x.shape
    row = pl.BlockSpec((tm, d), lambda i: (i, 0))
    return pl.pallas_call(
        _pool_kernel,
        grid=(s // tm,),
        in_specs=[row, _resident((1, d)), _resident(w.shape), _resident((1, d))],
        out_specs=row,
        out_shape=jax.ShapeDtypeStruct((s, d), F32),
        scratch_shapes=[pltpu.VMEM((tm + POOL_HALO, d), F32)],
        compiler_params=_params(),
        name="pool_mixer",
    )(x, gain, w, scale)


def kernel(x, ln_gains, ffn_w_gate, ffn_w_up, ffn_w_down, attn_w_qkv, attn_w_o,
           pool_w, pool_scale, final_gain):
    b, s, d = x.shape
    assert b == 1
    depth = ln_gains.shape[0]
    slopes = LOG2E * jnp.exp2(-8.0 * jnp.arange(1, N_HEADS + 1, dtype=F32) / N_HEADS)
    fg = final_gain.reshape(1, d)
    y = x.reshape(s, d)
    for i in range(depth):
        gains = ln_gains[i].reshape(3, 1, d)

        def half_ffn(y, which, final_norm=False, i=i, gains=gains):
            return _ffn(y, gains[2 * which],
                        ffn_w_gate[i, which].astype(BF16),
                        ffn_w_up[i, which].astype(BF16),
                        ffn_w_down[i, which].astype(BF16),
                        fg, final_norm=final_norm)

        y = half_ffn(y, 0)
        m = i // 2
        if i % 2 == 0:
            w_qkv = attn_w_qkv[m]
            q, k, vt, kmean = _qkv(y, gains[1], w_qkv[:, :2 * d].astype(BF16),
                                   w_qkv[:, 2 * d:].T.astype(BF16))
            o = _attention(slopes, q, k, vt, kmean.reshape(-1, d))
            y = _oproj(y, o, attn_w_o[m].astype(BF16))
        else:
            y = _pool(y, gains[1], pool_w[m].astype(BF16),
                      pool_scale[m].reshape(1, d))
        y = half_ffn(y, 1, final_norm=(i == depth - 1))
    return y.reshape(b, s, d)
```

```python
import functools

import jax
import jax.numpy as jnp
from jax import lax
from jax.experimental import pallas as pl
from jax.experimental.pallas import tpu as pltpu

N_HEADS = 16
HEAD_DIM = 64
MOBA_BLOCK = 256
MOBA_TOPK = 3
POOL_WINDOWS = (2, 4, 8, 16)
EPS = 1e-6
NEG = -1e30
LOG2E = 1.4426950408889634
UNDERFLOW = 140.0
NORM_SLACK = 1.02
NORM_CHUNK = 2048
PAIRS_PER_ITER = 8

LANES = 128
HEADS_PER_STEP = LANES // HEAD_DIM
ONES_ROWS = 16
POOL_HALO = 16
VMEM_LIMIT = 56 * 1024 * 1024

F32 = jnp.float32
BF16 = jnp.bfloat16


def _params():
    return pltpu.CompilerParams(
        dimension_semantics=("arbitrary",), vmem_limit_bytes=VMEM_LIMIT)


def _resident(shape):
    return pl.BlockSpec(shape, lambda *_: (0,) * len(shape),
                        pipeline_mode=pl.Buffered(1))


def _rmsnorm(x, g):
    ms = jnp.mean(x * x, axis=-1, keepdims=True)
    return x * lax.rsqrt(ms + EPS) * g


def _ffn_kernel(x_ref, g_ref, wg_ref, wu_ref, wd_ref, fg_ref, o_ref, a_ref, *,
                ff_chunk, final_norm):
    x = x_ref[...]
    h = _rmsnorm(x, g_ref[...]).astype(BF16)
    d_ff = wg_ref.shape[1]
    for c in range(d_ff // ff_chunk):
        sl = slice(c * ff_chunk, (c + 1) * ff_chunk)
        gate = jnp.dot(h, wg_ref[:, sl], preferred_element_type=F32)
        up = jnp.dot(h, wu_ref[:, sl], preferred_element_type=F32)
        a_ref[:, sl] = (gate * jax.nn.sigmoid(gate) * up).astype(BF16)
    y = x + 0.5 * jnp.dot(a_ref[...], wd_ref[...], preferred_element_type=F32)
    if final_norm:
        y = _rmsnorm(y, fg_ref[...])
    o_ref[...] = y


def _ffn(x, gain, wg, wu, wd, final_gain, *, final_norm, tm=512, ff_chunk=256):
    s, d = x.shape
    d_ff = wg.shape[1]
    row = pl.BlockSpec((tm, d), lambda i: (i, 0))
    return pl.pallas_call(
        functools.partial(_ffn_kernel, ff_chunk=ff_chunk, final_norm=final_norm),
        grid=(s // tm,),
        in_specs=[row, _resident((1, d)), _resident((d, d_ff)),
                  _resident((d, d_ff)), _resident((d_ff, d)), _resident((1, d))],
        out_specs=row,
        out_shape=jax.ShapeDtypeStruct((s, d), F32),
        scratch_shapes=[pltpu.VMEM((tm, d_ff), BF16)],
        compiler_params=_params(),
        name="ffn",
    )(x, gain, wg, wu, wd, final_gain)


def _qkv_kernel(x_ref, g_ref, wqk_ref, wvt_ref, q_ref, k_ref, vt_ref, km_ref):
    d = x_ref.shape[1]
    h = _rmsnorm(x_ref[...], g_ref[...]).astype(BF16)
    qk = jnp.dot(h, wqk_ref[...], preferred_element_type=F32)
    q_ref[...] = (qk[:, :d] * (HEAD_DIM ** -0.5 * LOG2E)).astype(BF16)
    k = qk[:, d:]
    k_ref[...] = k.astype(BF16)
    vt = lax.dot_general(wvt_ref[...], h, (((1,), (1,)), ((), ())),
                         preferred_element_type=F32).astype(BF16)
    for b in range(x_ref.shape[0] // MOBA_BLOCK):
        rows = slice(b * MOBA_BLOCK, (b + 1) * MOBA_BLOCK)
        km_ref[b] = jnp.mean(k[rows], axis=0, keepdims=True)
        vt_ref[b] = vt[:, rows]


def _qkv(x, gain, wqk, wvt, *, tm=512):
    s, d = x.shape
    row = pl.BlockSpec((tm, d), lambda i: (i, 0))
    nb_tile = tm // MOBA_BLOCK
    nb = s // MOBA_BLOCK
    return pl.pallas_call(
        _qkv_kernel,
        grid=(s // tm,),
        in_specs=[row, _resident((1, d)), _resident((d, 2 * d)), _resident((d, d))],
        out_specs=[row, row,
                   pl.BlockSpec((nb_tile, d, MOBA_BLOCK), lambda i: (i, 0, 0)),
                   pl.BlockSpec((nb_tile, 1, d), lambda i: (i, 0, 0))],
        out_shape=[jax.ShapeDtypeStruct((s, d), BF16)] * 2
        + [jax.ShapeDtypeStruct((nb, d, MOBA_BLOCK), BF16),
           jax.ShapeDtypeStruct((nb, 1, d), F32)],
        compiler_params=_params(),
        name="qkv",
    )(x, gain, wqk, wvt)


def _attn_kernel(slopes_ref, q_ref, k_ref, vt_ref, km_ref, o_ref,
                 bias_ref, add_ref, kn_ref, s0_ref, s1_ref, p0_ref, p1_ref):
    pair = pl.program_id(0)
    jq = pl.program_id(1)
    bs = MOBA_BLOCK
    nb = km_ref.shape[0]
    nt = (((1,), (1,)), ((), ()))
    heads = range(HEADS_PER_STEP)
    halves = [slice(h * LANES, (h + 1) * LANES) for h in range(bs // LANES)]
    s_refs = (s0_ref, s1_ref)
    p_refs = (p0_ref, p1_ref)
    q = q_ref[...]
    lane = lax.broadcasted_iota(jnp.int32, (1, LANES), 1)
    rel = (lax.broadcasted_iota(jnp.int32, (bs, bs), 1)
           - lax.broadcasted_iota(jnp.int32, (bs, bs), 0))
    blk = lax.broadcasted_iota(jnp.int32, (nb, bs), 0)
    blkf = blk.astype(F32)
    past = blk < jq

    lane_head = (lax.broadcasted_iota(jnp.int32, (LANES, LANES), 0) // HEAD_DIM
                 == lax.broadcasted_iota(jnp.int32, (LANES, LANES), 1)).astype(BF16)

    def max_head_norm_sq(x):
        xf = x.astype(F32)
        sq = jnp.dot((xf * xf).astype(BF16), lane_head, preferred_element_type=F32)
        return jnp.max(sq, axis=0, keepdims=True)

    @pl.when(jq == 0)
    def _():
        for hh in heads:
            bias_ref[hh] = slopes_ref[pair * HEADS_PER_STEP + hh] * rel.astype(F32)
        kn = jnp.zeros((1, LANES), F32)
        for c in range(k_ref.shape[0] // NORM_CHUNK):
            kn = jnp.maximum(kn, max_head_norm_sq(k_ref[c * NORM_CHUNK:(c + 1) * NORM_CHUNK, :]))
        kn_ref[...] = kn

    qhs = []
    for hh in heads:
        in_head = (lane >= hh * HEAD_DIM) & (lane < (hh + 1) * HEAD_DIM)
        qhs.append(jnp.where(in_head, q, jnp.zeros_like(q)))

    def block_scores(n, hh):
        keys = k_ref[pl.ds(pl.multiple_of(n * bs, bs), bs), :]
        return lax.dot_general(keys, qhs[hh], nt, preferred_element_type=F32)

    ones = jnp.ones((ONES_ROWS, bs), BF16)

    def weighted_values(n, slot, hh):
        lhs = jnp.concatenate([vt_ref[n][hh * HEAD_DIM:(hh + 1) * HEAD_DIM, :], ones], axis=0)
        return jnp.dot(lhs, p_refs[slot][hh], preferred_element_type=F32)

    i_stop = (jq + 1) // 2
    n_top = 2 * i_stop - 1
    s_own = [block_scores(jq, hh) for hh in heads]
    for hh in heads:
        s_refs[0][hh] = block_scores(jnp.maximum(n_top, 0), hh)

    km = km_ref[...]
    km_hi = km.astype(BF16)
    rest = km - km_hi.astype(F32)
    km_mid = rest.astype(BF16)
    km_lo = (rest - km_mid.astype(F32)).astype(BF16)
    km_terms = jnp.concatenate([km_hi, km_mid, km_lo], axis=0)
    for hh in heads:
        g3 = lax.dot_general(km_terms, qhs[hh], nt, preferred_element_type=F32)
        gate = g3[:nb] + g3[nb:2 * nb] + g3[2 * nb:]
        gate = jnp.where(past, gate, NEG)
        sel = jnp.zeros((nb, bs), jnp.bool_)
        for _ in range(MOBA_TOPK):
            top = jnp.max(gate, axis=0, keepdims=True)
            first = jnp.min(jnp.where(gate == top, blkf, float(nb)), axis=0, keepdims=True)
            pick = blkf == first
            sel = sel | (pick & past)
            gate = jnp.where(pick, -jnp.inf, gate)
        block_bias = (-slopes_ref[pair * HEADS_PER_STEP + hh] * bs) * (jq - blk).astype(F32)
        add_ref[hh] = jnp.where(sel, block_bias, NEG)

    causal = jnp.where(rel >= 0, 0.0, NEG)
    init = []
    for hh in heads:
        m = []
        for cols in halves:
            sb = s_own[hh][:, cols] - bias_ref[hh, :, cols] + causal[:, cols]
            m.append(jnp.max(sb, axis=0, keepdims=True))
            p_refs[1][hh, :, cols] = jnp.exp2(sb - m[-1]).astype(BF16)
        init.append((tuple(m), jnp.zeros((1, bs), F32), jnp.zeros((HEAD_DIM, bs), F32),
                     jnp.ones((1, bs), F32)))

    qk_bound = jnp.sqrt(max_head_norm_sq(q) * kn_ref[...] * NORM_SLACK)
    head_pairs = []
    for hh in heads:
        m_min = jnp.min(jnp.minimum(*init[hh][0]), axis=1, keepdims=True)
        reach = (UNDERFLOW + qk_bound[:, hh:hh + 1] - m_min) / slopes_ref[pair * HEADS_PER_STEP + hh]
        near = jq.astype(F32) - jnp.floor((reach - 1.0) / bs) - 1.0
        near = jnp.where(near > 0.0, near, 0.0)
        head_pairs.append(i_stop - near[0, 0].astype(jnp.int32) // 2)

    def step(n, n_prev, cur, carry, active):
        folded = {}
        for hh in active:
            _, l, acc, alpha_prev = carry[hh]
            r = weighted_values(n_prev, 1 - cur, hh)
            folded[hh] = (alpha_prev * l + r[HEAD_DIM:HEAD_DIM + 1],
                          alpha_prev * acc + r[:HEAD_DIM])
        n_next = jnp.maximum(n - 1, 0)
        for hh in active:
            s_refs[1 - cur][hh] = block_scores(n_next, hh)
        out = list(carry)
        for hh in active:
            m = carry[hh][0]
            row = add_ref[hh, pl.ds(n, 1), :]
            m_new, alpha = [], []
            for h, cols in enumerate(halves):
                sb = s_refs[cur][hh, :, cols] - bias_ref[hh, :, cols]
                top = jnp.max(sb, axis=0, keepdims=True) + row[:, cols]
                m_new.append(jnp.maximum(m[h], top))
                p_refs[cur][hh, :, cols] = jnp.exp2(sb - (m_new[h] - row[:, cols])).astype(BF16)
                alpha.append(jnp.exp2(m[h] - m_new[h]))
            out[hh] = (tuple(m_new), *folded[hh], jnp.concatenate(alpha, axis=1))
        return tuple(out)

    def pairs(t0, count, carry, active=tuple(heads)):
        for t in range(count):
            n = n_top - 2 * (t0 + t)
            carry = step(n, jnp.where(n == n_top, jq, n + 1), 0, carry, active)
            carry = step(n - 1, n, 1, carry, active)
        return carry

    n_pairs = functools.reduce(jnp.minimum, head_pairs)
    n_long = n_pairs // PAIRS_PER_ITER
    fin = lax.fori_loop(0, n_long,
                        lambda j, c: pairs(PAIRS_PER_ITER * j, PAIRS_PER_ITER, c), tuple(init))
    done = PAIRS_PER_ITER * n_long
    size = PAIRS_PER_ITER // 2
    while size:
        take = ((n_pairs - done) // size) % 2
        fin = lax.fori_loop(0, take, lambda _, c, done=done, size=size: pairs(done, size, c), fin)
        done = done + take * size
        size //= 2
    for hh in heads:
        fin = lax.fori_loop(n_pairs, head_pairs[hh],
                            lambda t, c, hh=hh: pairs(t, 1, c, (hh,)), fin)
    o_t = []
    for hh in heads:
        _, l, acc, alpha = fin[hh]
        n_last = jnp.where(head_pairs[hh] == 0, jq, n_top - 2 * head_pairs[hh] + 1)
        r = weighted_values(n_last, 1, hh)
        o_t.append((alpha * acc + r[:HEAD_DIM]) / (alpha * l + r[HEAD_DIM:HEAD_DIM + 1]))
    o_ref[...] = jnp.concatenate(o_t, axis=0).T.astype(BF16)


def _attention(slopes, q, k, vt, kmean):
    s, d = q.shape
    nb = kmean.shape[0]
    tile = pl.BlockSpec((MOBA_BLOCK, LANES), lambda p, j: (j, p))
    return pl.pallas_call(
        _attn_kernel,
        grid=(d // LANES, s // MOBA_BLOCK),
        in_specs=[pl.BlockSpec(memory_space=pltpu.SMEM), tile,
                  pl.BlockSpec((s, LANES), lambda p, j: (0, p)),
                  pl.BlockSpec((nb, LANES, MOBA_BLOCK), lambda p, j: (0, p, 0)),
                  pl.BlockSpec((nb, LANES), lambda p, j: (0, p))],
        out_specs=tile,
        out_shape=jax.ShapeDtypeStruct((s, d), BF16),
        scratch_shapes=[pltpu.VMEM((HEADS_PER_STEP, MOBA_BLOCK, MOBA_BLOCK), F32),
                        pltpu.VMEM((HEADS_PER_STEP, nb, MOBA_BLOCK), F32),
                        pltpu.VMEM((1, LANES), F32),
                        pltpu.VMEM((HEADS_PER_STEP, MOBA_BLOCK, MOBA_BLOCK), F32),
                        pltpu.VMEM((HEADS_PER_STEP, MOBA_BLOCK, MOBA_BLOCK), F32),
                        pltpu.VMEM((HEADS_PER_STEP, MOBA_BLOCK, MOBA_BLOCK), BF16),
                        pltpu.VMEM((HEADS_PER_STEP, MOBA_BLOCK, MOBA_BLOCK), BF16)],
        compiler_params=pltpu.CompilerParams(
            dimension_semantics=("arbitrary", "arbitrary"),
            vmem_limit_bytes=VMEM_LIMIT),
        name="moba_attention",
    )(slopes, q, k, vt, kmean)


def _oproj_kernel(x_ref, o_ref, w_ref, y_ref):
    y_ref[...] = x_ref[...] + jnp.dot(o_ref[...], w_ref[...],
                                      preferred_element_type=F32)


def _oproj(x, o, w, *, tm=512):
    s, d = x.shape
    row = pl.BlockSpec((tm, d), lambda i: (i, 0))
    return pl.pallas_call(
        _oproj_kernel,
        grid=(s // tm,),
        in_specs=[row, row, _resident((d, d))],
        out_specs=row,
        out_shape=jax.ShapeDtypeStruct((s, d), F32),
        compiler_params=_params(),
        name="attn_out_proj",
    )(x, o, w)


def _pool_kernel(x_ref, g_ref, w_ref, sc_ref, o_ref, h_ref):
    i = pl.program_id(0)
    tm, d = x_ref.shape
    gd = d // len(POOL_WINDOWS)
    x = x_ref[...]

    @pl.when(i == 0)
    def _():
        h_ref[0:POOL_HALO, :] = jnp.zeros((POOL_HALO, d), F32)

    h_ref[POOL_HALO:, :] = _rmsnorm(x, g_ref[...])
    t = i * tm + lax.broadcasted_iota(jnp.int32, (tm, 1), 0)
    for g, w in enumerate(POOL_WINDOWS):
        cols = slice(g * gd, (g + 1) * gd)
        win = h_ref[POOL_HALO:, cols]
        for back in range(1, w):
            win = win + h_ref[POOL_HALO - back:POOL_HALO - back + tm, cols]
        cnt = jnp.minimum(t + 1, w).astype(F32)
        y = (win / cnt - h_ref[POOL_HALO:, cols]).astype(BF16)
        mixed = jnp.dot(y, w_ref[g], preferred_element_type=F32)
        o_ref[:, cols] = x[:, cols] + mixed * sc_ref[:, cols]
    h_ref[0:POOL_HALO, :] = h_ref[tm:tm + POOL_HALO, :]


def _pool(x, gain, w, scale, *, tm=512):
    s, d = x.shape
    row = pl.BlockSpec((tm, d), lambda i: (i, 0))
    return pl.pallas_call(
        _pool_kernel,
        grid=(s // tm,),
        in_specs=[row, _resident((1, d)), _resident(w.shape), _resident((1, d))],
        out_specs=row,
        out_shape=jax.ShapeDtypeStruct((s, d), F32),
        scratch_shapes=[pltpu.VMEM((tm + POOL_HALO, d), F32)],
        compiler_params=_params(),
        name="pool_mixer",
    )(x, gain, w, scale)


def kernel(x, ln_gains, ffn_w_gate, ffn_w_up, ffn_w_down, attn_w_qkv, attn_w_o,
           pool_w, pool_scale, final_gain):
    b, s, d = x.shape
    assert b == 1
    depth = ln_gains.shape[0]
    slopes = LOG2E * jnp.exp2(-8.0 * jnp.arange(1, N_HEADS + 1, dtype=F32) / N_HEADS)
    fg = final_gain.reshape(1, d)
    y = x.reshape(s, d)
    for i in range(depth):
        gains = ln_gains[i].reshape(3, 1, d)

        def half_ffn(y, which, final_norm=False, i=i, gains=gains):
            return _ffn(y, gains[2 * which],
                        ffn_w_gate[i, which].astype(BF16),
                        ffn_w_up[i, which].astype(BF16),
                        ffn_w_down[i, which].astype(BF16),
                        fg, final_norm=final_norm)

        y = half_ffn(y, 0)
        m = i // 2
        if i % 2 == 0:
            w_qkv = attn_w_qkv[m]
            q, k, vt, kmean = _qkv(y, gains[1], w_qkv[:, :2 * d].astype(BF16),
                                   w_qkv[:, 2 * d:].T.astype(BF16))
            o = _attention(slopes, q, k, vt, kmean.reshape(-1, d))
            y = _oproj(y, o, attn_w_o[m].astype(BF16))
        else:
            y = _pool(y, gains[1], pool_w[m].astype(BF16),
                      pool_scale[m].reshape(1, d))
        y = half_ffn(y, 1, final_norm=(i == depth - 1))
    return y.reshape(b, s, d)
```

```python
import functools

import jax
import jax.numpy as jnp
from jax import lax
from jax.experimental import pallas as pl
from jax.experimental.pallas import tpu as pltpu

N_HEADS = 16
HEAD_DIM = 64
MOBA_BLOCK = 256
MOBA_TOPK = 3
POOL_WINDOWS = (2, 4, 8, 16)
EPS = 1e-6
NEG = -1e30
LOG2E = 1.4426950408889634
UNDERFLOW = 140.0
NORM_SLACK = 1.02
NORM_CHUNK = 2048
PAIRS_PER_ITER = 8

LANES = 128
HEADS_PER_STEP = LANES // HEAD_DIM
ONES_ROWS = 16
POOL_HALO = 16
VMEM_LIMIT = 56 * 1024 * 1024

F32 = jnp.float32
BF16 = jnp.bfloat16


def _params():
    return pltpu.CompilerParams(
        dimension_semantics=("arbitrary",), vmem_limit_bytes=VMEM_LIMIT)


def _resident(shape):
    return pl.BlockSpec(shape, lambda *_: (0,) * len(shape),
                        pipeline_mode=pl.Buffered(1))


def _rmsnorm(x, g):
    ms = jnp.mean(x * x, axis=-1, keepdims=True)
    return x * lax.rsqrt(ms + EPS) * g


def _ffn_kernel(x_ref, g_ref, wg_ref, wu_ref, wd_ref, fg_ref, o_ref, a_ref, *,
                ff_chunk, final_norm):
    x = x_ref[...]
    h = _rmsnorm(x, g_ref[...]).astype(BF16)
    d_ff = wg_ref.shape[1]
    for c in range(d_ff // ff_chunk):
        sl = slice(c * ff_chunk, (c + 1) * ff_chunk)
        gate = jnp.dot(h, wg_ref[:, sl], preferred_element_type=F32)
        up = jnp.dot(h, wu_ref[:, sl], preferred_element_type=F32)
        a_ref[:, sl] = (gate * jax.nn.sigmoid(gate) * up).astype(BF16)
    y = x + 0.5 * jnp.dot(a_ref[...], wd_ref[...], preferred_element_type=F32)
    if final_norm:
        y = _rmsnorm(y, fg_ref[...])
    o_ref[...] = y


def _ffn(x, gain, wg, wu, wd, final_gain, *, final_norm, tm=512, ff_chunk=256):
    s, d = x.shape
    d_ff = wg.shape[1]
    row = pl.BlockSpec((tm, d), lambda i: (i, 0))
    return pl.pallas_call(
        functools.partial(_ffn_kernel, ff_chunk=ff_chunk, final_norm=final_norm),
        grid=(s // tm,),
        in_specs=[row, _resident((1, d)), _resident((d, d_ff)),
                  _resident((d, d_ff)), _resident((d_ff, d)), _resident((1, d))],
        out_specs=row,
        out_shape=jax.ShapeDtypeStruct((s, d), F32),
        scratch_shapes=[pltpu.VMEM((tm, d_ff), BF16)],
        compiler_params=_params(),
        name="ffn",
    )(x, gain, wg, wu, wd, final_gain)


def _qkv_kernel(x_ref, g_ref, wqk_ref, wvt_ref, q_ref, k_ref, vt_ref, km_ref):
    d = x_ref.shape[1]
    h = _rmsnorm(x_ref[...], g_ref[...]).astype(BF16)
    qk = jnp.dot(h, wqk_ref[...], preferred_element_type=F32)
    q_ref[...] = (qk[:, :d] * (HEAD_DIM ** -0.5 * LOG2E)).astype(BF16)
    k = qk[:, d:]
    k_ref[...] = k.astype(BF16)
    vt = lax.dot_general(wvt_ref[...], h, (((1,), (1,)), ((), ())),
                         preferred_element_type=F32).astype(BF16)
    for b in range(x_ref.shape[0] // MOBA_BLOCK):
        rows = slice(b * MOBA_BLOCK, (b + 1) * MOBA_BLOCK)
        km_ref[b] = jnp.mean(k[rows], axis=0, keepdims=True)
        vt_ref[b] = vt[:, rows]


def _qkv(x, gain, wqk, wvt, *, tm=512):
    s, d = x.shape
    row = pl.BlockSpec((tm, d), lambda i: (i, 0))
    nb_tile = tm // MOBA_BLOCK
    nb = s // MOBA_BLOCK
    return pl.pallas_call(
        _qkv_kernel,
        grid=(s // tm,),
        in_specs=[row, _resident((1, d)), _resident((d, 2 * d)), _resident((d, d))],
        out_specs=[row, row,
                   pl.BlockSpec((nb_tile, d, MOBA_BLOCK), lambda i: (i, 0, 0)),
                   pl.BlockSpec((nb_tile, 1, d), lambda i: (i, 0, 0))],
        out_shape=[jax.ShapeDtypeStruct((s, d), BF16)] * 2
        + [jax.ShapeDtypeStruct((nb, d, MOBA_BLOCK), BF16),
           jax.ShapeDtypeStruct((nb, 1, d), F32)],
        compiler_params=_params(),
        name="qkv",
    )(x, gain, wqk, wvt)


def _attn_kernel(slopes_ref, q_ref, k_ref, vt_ref, km_ref, o_ref,
                 bias_ref, add_ref, kn_ref, s0_ref, s1_ref, p0_ref, p1_ref):
    pair = pl.program_id(0)
    jq = pl.program_id(1)
    bs = MOBA_BLOCK
    nb = km_ref.shape[0]
    heads = range(HEADS_PER_STEP)
    halves = [slice(h * LANES, (h + 1) * LANES) for h in range(bs // LANES)]
    s_refs = (s0_ref, s1_ref)
    p_refs = (p0_ref, p1_ref)
    q = q_ref[...]
    lane = lax.broadcasted_iota(jnp.int32, (1, LANES), 1)
    rel = (lax.broadcasted_iota(jnp.int32, (bs, bs), 1)
           - lax.broadcasted_iota(jnp.int32, (bs, bs), 0))
    blk = lax.broadcasted_iota(jnp.int32, (nb, bs), 0)
    blkf = blk.astype(F32)
    past = blk < jq

    lane_head = (lax.broadcasted_iota(jnp.int32, (LANES, LANES), 0) // HEAD_DIM
                 == lax.broadcasted_iota(jnp.int32, (LANES, LANES), 1)).astype(BF16)

    def max_head_norm_sq(x):
        xf = x.astype(F32)
        sq = jnp.dot((xf * xf).astype(BF16), lane_head, preferred_element_type=F32)
        return jnp.max(sq, axis=0, keepdims=True)

    @pl.when(jq == 0)
    def _():
        for hh in heads:
            bias_ref[hh] = slopes_ref[pair * HEADS_PER_STEP + hh] * rel.astype(F32)
        kn = jnp.zeros((1, LANES), F32)
        for c in range(k_ref.shape[0] // NORM_CHUNK):
            kn = jnp.maximum(kn, max_head_norm_sq(k_ref[c * NORM_CHUNK:(c + 1) * NORM_CHUNK, :]))
        kn_ref[...] = kn

    qhs = []
    for hh in heads:
        in_head = (lane >= hh * HEAD_DIM) & (lane < (hh + 1) * HEAD_DIM)
        qh = jnp.where(in_head, q, jnp.zeros_like(q))
        qhs.append(qh.astype(F32).T.astype(BF16))

    def block_scores(n, hh):
        keys = k_ref[pl.ds(pl.multiple_of(n * bs, bs), bs), :]
        return jnp.dot(keys, qhs[hh], preferred_element_type=F32)

    ones = jnp.ones((ONES_ROWS, bs), BF16)

    def weighted_values(n, slot, hh):
        lhs = jnp.concatenate([vt_ref[n][hh * HEAD_DIM:(hh + 1) * HEAD_DIM, :], ones], axis=0)
        return jnp.dot(lhs, p_refs[slot][hh], preferred_element_type=F32)

    i_stop = (jq + 1) // 2
    n_top = 2 * i_stop - 1
    s_own = [block_scores(jq, hh) for hh in heads]
    for hh in heads:
        s_refs[0][hh] = block_scores(jnp.maximum(n_top, 0), hh)

    km = km_ref[...]
    km_hi = km.astype(BF16)
    rest = km - km_hi.astype(F32)
    km_mid = rest.astype(BF16)
    km_lo = (rest - km_mid.astype(F32)).astype(BF16)
    km_terms = jnp.concatenate([km_hi, km_mid, km_lo], axis=0)
    for hh in heads:
        g3 = jnp.dot(km_terms, qhs[hh], preferred_element_type=F32)
        gate = g3[:nb] + g3[nb:2 * nb] + g3[2 * nb:]
        gate = jnp.where(past, gate, NEG)
        sel = jnp.zeros((nb, bs), jnp.bool_)
        for _ in range(MOBA_TOPK):
            top = jnp.max(gate, axis=0, keepdims=True)
            first = jnp.min(jnp.where(gate == top, blkf, float(nb)), axis=0, keepdims=True)
            pick = blkf == first
            sel = sel | (pick & past)
            gate = jnp.where(pick, -jnp.inf, gate)
        block_bias = (-slopes_ref[pair * HEADS_PER_STEP + hh] * bs) * (jq - blk).astype(F32)
        add_ref[hh] = jnp.where(sel, block_bias, NEG)

    causal = jnp.where(rel >= 0, 0.0, NEG)
    init = []
    for hh in heads:
        m = []
        for cols in halves:
            sb = s_own[hh][:, cols] - bias_ref[hh, :, cols] + causal[:, cols]
            m.append(jnp.max(sb, axis=0, keepdims=True))
            p_refs[1][hh, :, cols] = jnp.exp2(sb - m[-1]).astype(BF16)
        init.append((tuple(m), jnp.zeros((1, bs), F32), jnp.zeros((HEAD_DIM, bs), F32),
                     jnp.ones((1, bs), F32)))

    qk_bound = jnp.sqrt(max_head_norm_sq(q) * kn_ref[...] * NORM_SLACK)
    first_block = None
    for hh in heads:
        m_min = jnp.min(jnp.minimum(*init[hh][0]), axis=1, keepdims=True)
        reach = (UNDERFLOW + qk_bound[:, hh:hh + 1] - m_min) / slopes_ref[pair * HEADS_PER_STEP + hh]
        near = jq.astype(F32) - jnp.floor((reach - 1.0) / bs) - 1.0
        near = jnp.where(near > 0.0, near, 0.0)
        first_block = near if first_block is None else jnp.minimum(first_block, near)
    i_start = first_block[0, 0].astype(jnp.int32) // 2

    def step(n, n_prev, cur, carry):
        folded = []
        for hh in heads:
            _, l, acc, alpha_prev = carry[hh]
            r = weighted_values(n_prev, 1 - cur, hh)
            folded.append((alpha_prev * l + r[HEAD_DIM:HEAD_DIM + 1],
                           alpha_prev * acc + r[:HEAD_DIM]))
        n_next = jnp.maximum(n - 1, 0)
        for hh in heads:
            s_refs[1 - cur][hh] = block_scores(n_next, hh)
        out = []
        for hh in heads:
            m = carry[hh][0]
            row = add_ref[hh, pl.ds(n, 1), :]
            m_new, alpha = [], []
            for h, cols in enumerate(halves):
                sb = s_refs[cur][hh, :, cols] - bias_ref[hh, :, cols]
                top = jnp.max(sb, axis=0, keepdims=True) + row[:, cols]
                m_new.append(jnp.maximum(m[h], top))
                p_refs[cur][hh, :, cols] = jnp.exp2(sb - (m_new[h] - row[:, cols])).astype(BF16)
                alpha.append(jnp.exp2(m[h] - m_new[h]))
            out.append((tuple(m_new), *folded[hh], jnp.concatenate(alpha, axis=1)))
        return tuple(out)

    def pairs(t0, count, carry):
        for t in range(count):
            n = n_top - 2 * (t0 + t)
            carry = step(n, jnp.where(n == n_top, jq, n + 1), 0, carry)
            carry = step(n - 1, n, 1, carry)
        return carry

    n_pairs = i_stop - i_start
    n_long = n_pairs // PAIRS_PER_ITER
    fin = lax.fori_loop(0, n_long,
                        lambda j, c: pairs(PAIRS_PER_ITER * j, PAIRS_PER_ITER, c), tuple(init))
    done = PAIRS_PER_ITER * n_long
    size = PAIRS_PER_ITER // 2
    while size:
        take = ((n_pairs - done) // size) % 2
        fin = lax.fori_loop(0, take, lambda _, c, done=done, size=size: pairs(done, size, c), fin)
        done = done + take * size
        size //= 2
    n_last = jnp.where(n_pairs == 0, jq, 2 * i_start)
    o_t = []
    for hh in heads:
        _, l, acc, alpha = fin[hh]
        r = weighted_values(n_last, 1, hh)
        o_t.append((alpha * acc + r[:HEAD_DIM]) / (alpha * l + r[HEAD_DIM:HEAD_DIM + 1]))
    o_ref[...] = jnp.concatenate(o_t, axis=0).T.astype(BF16)


def _attention(slopes, q, k, vt, kmean):
    s, d = q.shape
    nb = kmean.shape[0]
    tile = pl.BlockSpec((MOBA_BLOCK, LANES), lambda p, j: (j, p))
    return pl.pallas_call(
        _attn_kernel,
        grid=(d // LANES, s // MOBA_BLOCK),
        in_specs=[pl.BlockSpec(memory_space=pltpu.SMEM), tile,
                  pl.BlockSpec((s, LANES), lambda p, j: (0, p)),
                  pl.BlockSpec((nb, LANES, MOBA_BLOCK), lambda p, j: (0, p, 0)),
                  pl.BlockSpec((nb, LANES), lambda p, j: (0, p))],
        out_specs=tile,
        out_shape=jax.ShapeDtypeStruct((s, d), BF16),
        scratch_shapes=[pltpu.VMEM((HEADS_PER_STEP, MOBA_BLOCK, MOBA_BLOCK), F32),
                        pltpu.VMEM((HEADS_PER_STEP, nb, MOBA_BLOCK), F32),
                        pltpu.VMEM((1, LANES), F32),
                        pltpu.VMEM((HEADS_PER_STEP, MOBA_BLOCK, MOBA_BLOCK), F32),
                        pltpu.VMEM((HEADS_PER_STEP, MOBA_BLOCK, MOBA_BLOCK), F32),
                        pltpu.VMEM((HEADS_PER_STEP, MOBA_BLOCK, MOBA_BLOCK), BF16),
                        pltpu.VMEM((HEADS_PER_STEP, MOBA_BLOCK, MOBA_BLOCK), BF16)],
        compiler_params=pltpu.CompilerParams(
            dimension_semantics=("arbitrary", "arbitrary"),
            vmem_limit_bytes=VMEM_LIMIT),
        name="moba_attention",
    )(slopes, q, k, vt, kmean)


def _oproj_kernel(x_ref, o_ref, w_ref, y_ref):
    y_ref[...] = x_ref[...] + jnp.dot(o_ref[...], w_ref[...],
                                      preferred_element_type=F32)


def _oproj(x, o, w, *, tm=512):
    s, d = x.shape
    row = pl.BlockSpec((tm, d), lambda i: (i, 0))
    return pl.pallas_call(
        _oproj_kernel,
        grid=(s // tm,),
        in_specs=[row, row, _resident((d, d))],
        out_specs=row,
        out_shape=jax.ShapeDtypeStruct((s, d), F32),
        compiler_params=_params(),
        name="attn_out_proj",
    )(x, o, w)


def _pool_kernel(x_ref, g_ref, w_ref, sc_ref, o_ref, h_ref):
    i = pl.program_id(0)
    tm, d = x_ref.shape
    gd = d // len(POOL_WINDOWS)
    x = x_ref[...]

    @pl.when(i == 0)
    def _():
        h_ref[0:POOL_HALO, :] = jnp.zeros((POOL_HALO, d), F32)

    h_ref[POOL_HALO:, :] = _rmsnorm(x, g_ref[...])
    t = i * tm + lax.broadcasted_iota(jnp.int32, (tm, 1), 0)
    for g, w in enumerate(POOL_WINDOWS):
        cols = slice(g * gd, (g + 1) * gd)
        win = h_ref[POOL_HALO:, cols]
        for back in range(1, w):
            win = win + h_ref[POOL_HALO - back:POOL_HALO - back + tm, cols]
        cnt = jnp.minimum(t + 1, w).astype(F32)
        y = (win / cnt - h_ref[POOL_HALO:, cols]).astype(BF16)
        mixed = jnp.dot(y, w_ref[g], preferred_element_type=F32)
        o_ref[:, cols] = x[:, cols] + mixed * sc_ref[:, cols]
    h_ref[0:POOL_HALO, :] = h_ref[tm:tm + POOL_HALO, :]


def _pool(x, gain, w, scale, *, tm=512):
    s, d = x.shape
    row = pl.BlockSpec((tm, d), lambda i: (i, 0))
    return pl.pallas_call(
        _pool_kernel,
        grid=(s // tm,),
        in_specs=[row, _resident((1, d)), _resident(w.shape), _resident((1, d))],
        out_specs=row,
        out_shape=jax.ShapeDtypeStruct((s, d), F32),
        scratch_shapes=[pltpu.VMEM((tm + POOL_HALO, d), F32)],
        compiler_params=_params(),
        name="pool_mixer",
    )(x, gain, w, scale)


def kernel(x, ln_gains, ffn_w_gate, ffn_w_up, ffn_w_down, attn_w_qkv, attn_w_o,
           pool_w, pool_scale, final_gain):
    b, s, d = x.shape
    assert b == 1
    depth = ln_gains.shape[0]
    slopes = LOG2E * jnp.exp2(-8.0 * jnp.arange(1, N_HEADS + 1, dtype=F32) / N_HEADS)
    fg = final_gain.reshape(1, d)
    y = x.reshape(s, d)
    for i in range(depth):
        gains = ln_gains[i].reshape(3, 1, d)

        def half_ffn(y, which, final_norm=False, i=i, gains=gains):
            return _ffn(y, gains[2 * which],
                        ffn_w_gate[i, which].astype(BF16),
                        ffn_w_up[i, which].astype(BF16),
                        ffn_w_down[i, which].astype(BF16),
                        fg, final_norm=final_norm)

        y = half_ffn(y, 0)
        m = i // 2
        if i % 2 == 0:
            w_qkv = attn_w_qkv[m]
            q, k, vt, kmean = _qkv(y, gains[1], w_qkv[:, :2 * d].astype(BF16),
                                   w_qkv[:, 2 * d:].T.astype(BF16))
            o = _attention(slopes, q, k, vt, kmean.reshape(-1, d))
            y = _oproj(y, o, attn_w_o[m].astype(BF16))
        else:
            y = _pool(y, gains[1], pool_w[m].astype(BF16),
                      pool_scale[m].reshape(1, d))
        y = half_ffn(y, 1, final_norm=(i == depth - 1))
    return y.reshape(b, s, d)
```

```python
import functools

import jax
import jax.numpy as jnp
from jax import lax
from jax.experimental import pallas as pl
from jax.experimental.pallas import tpu as pltpu

N_HEADS = 16
HEAD_DIM = 64
MOBA_BLOCK = 256
MOBA_TOPK = 3
POOL_WINDOWS = (2, 4, 8, 16)
EPS = 1e-6
NEG = -1e30
LOG2E = 1.4426950408889634
UNDERFLOW = 140.0
NORM_SLACK = 1.02
NORM_CHUNK = 2048
PAIRS_PER_ITER = 8

LANES = 128
HEADS_PER_STEP = LANES // HEAD_DIM
ONES_ROWS = 16
POOL_LEVEL_STEP = 8
POOL_HALO = 32
VMEM_LIMIT = 56 * 1024 * 1024

F32 = jnp.float32
BF16 = jnp.bfloat16


def _params():
    return pltpu.CompilerParams(
        dimension_semantics=("arbitrary",), vmem_limit_bytes=VMEM_LIMIT)


def _resident(shape):
    return pl.BlockSpec(shape, lambda *_: (0,) * len(shape),
                        pipeline_mode=pl.Buffered(1))


def _rmsnorm(x, g):
    ms = jnp.mean(x * x, axis=-1, keepdims=True)
    return x * lax.rsqrt(ms + EPS) * g


def _ffn_kernel(x_ref, g_ref, wg_ref, wu_ref, wd_ref, fg_ref, o_ref, a_ref, *,
                ff_chunk, final_norm):
    x = x_ref[...]
    h = _rmsnorm(x, g_ref[...]).astype(BF16)
    d_ff = wg_ref.shape[1]
    for c in range(d_ff // ff_chunk):
        sl = slice(c * ff_chunk, (c + 1) * ff_chunk)
        gate = jnp.dot(h, wg_ref[:, sl], preferred_element_type=F32)
        up = jnp.dot(h, wu_ref[:, sl], preferred_element_type=F32)
        a_ref[:, sl] = (gate * jax.nn.sigmoid(gate) * up).astype(BF16)
    y = x + 0.5 * jnp.dot(a_ref[...], wd_ref[...], preferred_element_type=F32)
    if final_norm:
        y = _rmsnorm(y, fg_ref[...])
    o_ref[...] = y


def _ffn(x, gain, wg, wu, wd, final_gain, *, final_norm, tm=512, ff_chunk=256):
    s, d = x.shape
    d_ff = wg.shape[1]
    row = pl.BlockSpec((tm, d), lambda i: (i, 0))
    return pl.pallas_call(
        functools.partial(_ffn_kernel, ff_chunk=ff_chunk, final_norm=final_norm),
        grid=(s // tm,),
        in_specs=[row, _resident((1, d)), _resident((d, d_ff)),
                  _resident((d, d_ff)), _resident((d_ff, d)), _resident((1, d))],
        out_specs=row,
        out_shape=jax.ShapeDtypeStruct((s, d), F32),
        scratch_shapes=[pltpu.VMEM((tm, d_ff), BF16)],
        compiler_params=_params(),
        name="ffn",
    )(x, gain, wg, wu, wd, final_gain)


def _qkv_kernel(x_ref, g_ref, wqk_ref, wvt_ref, q_ref, k_ref, vt_ref, km_ref):
    d = x_ref.shape[1]
    h = _rmsnorm(x_ref[...], g_ref[...]).astype(BF16)
    qk = jnp.dot(h, wqk_ref[...], preferred_element_type=F32)
    q_ref[...] = (qk[:, :d] * (HEAD_DIM ** -0.5 * LOG2E)).astype(BF16)
    k = qk[:, d:]
    k_ref[...] = k.astype(BF16)
    vt = lax.dot_general(wvt_ref[...], h, (((1,), (1,)), ((), ())),
                         preferred_element_type=F32).astype(BF16)
    for b in range(x_ref.shape[0] // MOBA_BLOCK):
        rows = slice(b * MOBA_BLOCK, (b + 1) * MOBA_BLOCK)
        km_ref[b] = jnp.mean(k[rows], axis=0, keepdims=True)
        vt_ref[b] = vt[:, rows]


def _qkv(x, gain, wqk, wvt, *, tm=512):
    s, d = x.shape
    row = pl.BlockSpec((tm, d), lambda i: (i, 0))
    nb_tile = tm // MOBA_BLOCK
    nb = s // MOBA_BLOCK
    return pl.pallas_call(
        _qkv_kernel,
        grid=(s // tm,),
        in_specs=[row, _resident((1, d)), _resident((d, 2 * d)), _resident((d, d))],
        out_specs=[row, row,
                   pl.BlockSpec((nb_tile, d, MOBA_BLOCK), lambda i: (i, 0, 0)),
                   pl.BlockSpec((nb_tile, 1, d), lambda i: (i, 0, 0))],
        out_shape=[jax.ShapeDtypeStruct((s, d), BF16)] * 2
        + [jax.ShapeDtypeStruct((nb, d, MOBA_BLOCK), BF16),
           jax.ShapeDtypeStruct((nb, 1, d), F32)],
        compiler_params=_params(),
        name="qkv",
    )(x, gain, wqk, wvt)


def _attn_kernel(slopes_ref, q_ref, k_ref, vt_ref, km_ref, o_ref,
                 bias_ref, add_ref, kn_ref, s0_ref, s1_ref, p0_ref, p1_ref):
    pair = pl.program_id(0)
    jq = pl.program_id(1)
    bs = MOBA_BLOCK
    nb = km_ref.shape[0]
    heads = range(HEADS_PER_STEP)
    halves = [slice(h * LANES, (h + 1) * LANES) for h in range(bs // LANES)]
    s_refs = (s0_ref, s1_ref)
    p_refs = (p0_ref, p1_ref)
    q = q_ref[...]
    lane = lax.broadcasted_iota(jnp.int32, (1, LANES), 1)
    rel = (lax.broadcasted_iota(jnp.int32, (bs, bs), 1)
           - lax.broadcasted_iota(jnp.int32, (bs, bs), 0))
    blk = lax.broadcasted_iota(jnp.int32, (nb, bs), 0)
    blkf = blk.astype(F32)
    past = blk < jq

    lane_head = (lax.broadcasted_iota(jnp.int32, (LANES, LANES), 0) // HEAD_DIM
                 == lax.broadcasted_iota(jnp.int32, (LANES, LANES), 1)).astype(BF16)

    def max_head_norm_sq(x):
        xf = x.astype(F32)
        sq = jnp.dot((xf * xf).astype(BF16), lane_head, preferred_element_type=F32)
        return jnp.max(sq, axis=0, keepdims=True)

    @pl.when(jq == 0)
    def _():
        for hh in heads:
            bias_ref[hh] = slopes_ref[pair * HEADS_PER_STEP + hh] * rel.astype(F32)
        kn = jnp.zeros((1, LANES), F32)
        for c in range(k_ref.shape[0] // NORM_CHUNK):
            kn = jnp.maximum(kn, max_head_norm_sq(k_ref[c * NORM_CHUNK:(c + 1) * NORM_CHUNK, :]))
        kn_ref[...] = kn

    qhs = []
    for hh in heads:
        in_head = (lane >= hh * HEAD_DIM) & (lane < (hh + 1) * HEAD_DIM)
        qh = jnp.where(in_head, q, jnp.zeros_like(q))
        qhs.append(qh.astype(F32).T.astype(BF16))

    def block_scores(n, hh):
        keys = k_ref[pl.ds(pl.multiple_of(n * bs, bs), bs), :]
        return jnp.dot(keys, qhs[hh], preferred_element_type=F32)

    ones = jnp.ones((ONES_ROWS, bs), BF16)

    def weighted_values(n, slot, hh):
        lhs = jnp.concatenate([vt_ref[n][hh * HEAD_DIM:(hh + 1) * HEAD_DIM, :], ones], axis=0)
        return jnp.dot(lhs, p_refs[slot][hh], preferred_element_type=F32)

    i_stop = (jq + 1) // 2
    n_top = 2 * i_stop - 1
    s_own = [block_scores(jq, hh) for hh in heads]
    for hh in heads:
        s_refs[0][hh] = block_scores(jnp.maximum(n_top, 0), hh)

    km = km_ref[...]
    km_hi = km.astype(BF16)
    rest = km - km_hi.astype(F32)
    km_mid = rest.astype(BF16)
    km_lo = (rest - km_mid.astype(F32)).astype(BF16)
    km_terms = jnp.concatenate([km_hi, km_mid, km_lo], axis=0)
    for hh in heads:
        g3 = jnp.dot(km_terms, qhs[hh], preferred_element_type=F32)
        gate = g3[:nb] + g3[nb:2 * nb] + g3[2 * nb:]
        gate = jnp.where(past, gate, NEG)
        sel = jnp.zeros((nb, bs), jnp.bool_)
        for _ in range(MOBA_TOPK):
            top = jnp.max(gate, axis=0, keepdims=True)
            first = jnp.min(jnp.where(gate == top, blkf, float(nb)), axis=0, keepdims=True)
            pick = blkf == first
            sel = sel | (pick & past)
            gate = jnp.where(pick, -jnp.inf, gate)
        block_bias = (-slopes_ref[pair * HEADS_PER_STEP + hh] * bs) * (jq - blk).astype(F32)
        add_ref[hh] = jnp.where(sel, block_bias, NEG)

    causal = jnp.where(rel >= 0, 0.0, NEG)
    init = []
    for hh in heads:
        m = []
        for cols in halves:
            sb = s_own[hh][:, cols] - bias_ref[hh, :, cols] + causal[:, cols]
            m.append(jnp.max(sb, axis=0, keepdims=True))
            p_refs[1][hh, :, cols] = jnp.exp2(sb - m[-1]).astype(BF16)
        init.append((tuple(m), jnp.zeros((1, bs), F32), jnp.zeros((HEAD_DIM, bs), F32),
                     jnp.ones((1, bs), F32)))

    qk_bound = jnp.sqrt(max_head_norm_sq(q) * kn_ref[...] * NORM_SLACK)
    first_block = None
    for hh in heads:
        m_min = jnp.min(jnp.minimum(*init[hh][0]), axis=1, keepdims=True)
        reach = (UNDERFLOW + qk_bound[:, hh:hh + 1] - m_min) / slopes_ref[pair * HEADS_PER_STEP + hh]
        near = jq.astype(F32) - jnp.floor((reach - 1.0) / bs) - 1.0
        near = jnp.where(near > 0.0, near, 0.0)
        first_block = near if first_block is None else jnp.minimum(first_block, near)
    i_start = first_block[0, 0].astype(jnp.int32) // 2

    def step(n, n_prev, cur, carry):
        folded = []
        for hh in heads:
            _, l, acc, alpha_prev = carry[hh]
            r = weighted_values(n_prev, 1 - cur, hh)
            folded.append((alpha_prev * l + r[HEAD_DIM:HEAD_DIM + 1],
                           alpha_prev * acc + r[:HEAD_DIM]))
        n_next = jnp.maximum(n - 1, 0)
        for hh in heads:
            s_refs[1 - cur][hh] = block_scores(n_next, hh)
        out = []
        for hh in heads:
            m = carry[hh][0]
            row = add_ref[hh, pl.ds(n, 1), :]
            m_new, alpha = [], []
            for h, cols in enumerate(halves):
                sb = s_refs[cur][hh, :, cols] - bias_ref[hh, :, cols]
                top = jnp.max(sb, axis=0, keepdims=True) + row[:, cols]
                m_new.append(jnp.maximum(m[h], top))
                p_refs[cur][hh, :, cols] = jnp.exp2(sb - (m_new[h] - row[:, cols])).astype(BF16)
                alpha.append(jnp.exp2(m[h] - m_new[h]))
            out.append((tuple(m_new), *folded[hh], jnp.concatenate(alpha, axis=1)))
        return tuple(out)

    def pairs(t0, count, carry):
        for t in range(count):
            n = n_top - 2 * (t0 + t)
            carry = step(n, jnp.where(n == n_top, jq, n + 1), 0, carry)
            carry = step(n - 1, n, 1, carry)
        return carry

    n_pairs = i_stop - i_start
    n_long = n_pairs // PAIRS_PER_ITER
    fin = lax.fori_loop(0, n_long,
                        lambda j, c: pairs(PAIRS_PER_ITER * j, PAIRS_PER_ITER, c), tuple(init))
    done = PAIRS_PER_ITER * n_long
    size = PAIRS_PER_ITER // 2
    while size:
        take = ((n_pairs - done) // size) % 2
        fin = lax.fori_loop(0, take, lambda _, c, done=done, size=size: pairs(done, size, c), fin)
        done = done + take * size
        size //= 2
    n_last = jnp.where(n_pairs == 0, jq, 2 * i_start)
    o_t = []
    for hh in heads:
        _, l, acc, alpha = fin[hh]
        r = weighted_values(n_last, 1, hh)
        o_t.append((alpha * acc + r[:HEAD_DIM]) / (alpha * l + r[HEAD_DIM:HEAD_DIM + 1]))
    o_ref[...] = jnp.concatenate(o_t, axis=0).T.astype(BF16)


def _attention(slopes, q, k, vt, kmean):
    s, d = q.shape
    nb = kmean.shape[0]
    tile = pl.BlockSpec((MOBA_BLOCK, LANES), lambda p, j: (j, p))
    return pl.pallas_call(
        _attn_kernel,
        grid=(d // LANES, s // MOBA_BLOCK),
        in_specs=[pl.BlockSpec(memory_space=pltpu.SMEM), tile,
                  pl.BlockSpec((s, LANES), lambda p, j: (0, p)),
                  pl.BlockSpec((nb, LANES, MOBA_BLOCK), lambda p, j: (0, p, 0)),
                  pl.BlockSpec((nb, LANES), lambda p, j: (0, p))],
        out_specs=tile,
        out_shape=jax.ShapeDtypeStruct((s, d), BF16),
        scratch_shapes=[pltpu.VMEM((HEADS_PER_STEP, MOBA_BLOCK, MOBA_BLOCK), F32),
                        pltpu.VMEM((HEADS_PER_STEP, nb, MOBA_BLOCK), F32),
                        pltpu.VMEM((1, LANES), F32),
                        pltpu.VMEM((HEADS_PER_STEP, MOBA_BLOCK, MOBA_BLOCK), F32),
                        pltpu.VMEM((HEADS_PER_STEP, MOBA_BLOCK, MOBA_BLOCK), F32),
                        pltpu.VMEM((HEADS_PER_STEP, MOBA_BLOCK, MOBA_BLOCK), BF16),
                        pltpu.VMEM((HEADS_PER_STEP, MOBA_BLOCK, MOBA_BLOCK), BF16)],
        compiler_params=pltpu.CompilerParams(
            dimension_semantics=("arbitrary", "arbitrary"),
            vmem_limit_bytes=VMEM_LIMIT),
        name="moba_attention",
    )(slopes, q, k, vt, kmean)


def _oproj_kernel(x_ref, o_ref, w_ref, y_ref):
    y_ref[...] = x_ref[...] + jnp.dot(o_ref[...], w_ref[...],
                                      preferred_element_type=F32)


def _oproj(x, o, w, *, tm=512):
    s, d = x.shape
    row = pl.BlockSpec((tm, d), lambda i: (i, 0))
    return pl.pallas_call(
        _oproj_kernel,
        grid=(s // tm,),
        in_specs=[row, row, _resident((d, d))],
        out_specs=row,
        out_shape=jax.ShapeDtypeStruct((s, d), F32),
        compiler_params=_params(),
        name="attn_out_proj",
    )(x, o, w)


def _pool_kernel(x_ref, g_ref, w_ref, sc_ref, o_ref, h_ref, lvl0_ref, lvl1_ref):
    i = pl.program_id(0)
    tm, d = x_ref.shape
    gd = d // len(POOL_WINDOWS)
    x = x_ref[...]
    lvl_refs = (lvl0_ref, lvl1_ref)
    end = tm + POOL_HALO

    @pl.when(i == 0)
    def _():
        h_ref[0:POOL_HALO, :] = jnp.zeros((POOL_HALO, d), F32)

    h_ref[POOL_HALO:, :] = _rmsnorm(x, g_ref[...])
    t = i * tm + lax.broadcasted_iota(jnp.int32, (tm, 1), 0)
    for g, w in enumerate(POOL_WINDOWS):
        cols = slice(g * gd, (g + 1) * gd)
        n_levels = w.bit_length() - 1
        src = None
        for j in range(1, n_levels + 1):
            shift = 2 ** (j - 1)
            start = POOL_HALO if j == n_levels else POOL_LEVEL_STEP * j
            if src is None:
                val = h_ref[start:, cols] + h_ref[start - shift:end - shift, cols]
            else:
                val = src[start:, :] + src[start - shift:end - shift, :]
            if j < n_levels:
                src = lvl_refs[j % 2]
                src[start:, :] = val
        win = val
        cnt = jnp.minimum(t + 1, w).astype(F32)
        y = (win / cnt - h_ref[POOL_HALO:, cols]).astype(BF16)
        mixed = jnp.dot(y, w_ref[g], preferred_element_type=F32)
        o_ref[:, cols] = x[:, cols] + mixed * sc_ref[:, cols]
    h_ref[0:POOL_HALO, :] = h_ref[tm:tm + POOL_HALO, :]


def _pool(x, gain, w, scale, *, tm=512):
    s, d = x.shape
    row = pl.BlockSpec((tm, d), lambda i: (i, 0))
    return pl.pallas_call(
        _pool_kernel,
        grid=(s // tm,),
        in_specs=[row, _resident((1, d)), _resident(w.shape), _resident((1, d))],
        out_specs=row,
        out_shape=jax.ShapeDtypeStruct((s, d), F32),
        scratch_shapes=[pltpu.VMEM((tm + POOL_HALO, d), F32)]
        + [pltpu.VMEM((tm + POOL_HALO, d // len(POOL_WINDOWS)), F32)] * 2,
        compiler_params=_params(),
        name="pool_mixer",
    )(x, gain, w, scale)


def kernel(x, ln_gains, ffn_w_gate, ffn_w_up, ffn_w_down, attn_w_qkv, attn_w_o,
           pool_w, pool_scale, final_gain):
    b, s, d = x.shape
    assert b == 1
    depth = ln_gains.shape[0]
    slopes = LOG2E * jnp.exp2(-8.0 * jnp.arange(1, N_HEADS + 1, dtype=F32) / N_HEADS)
    fg = final_gain.reshape(1, d)
    y = x.reshape(s, d)
    for i in range(depth):
        gains = ln_gains[i].reshape(3, 1, d)

        def half_ffn(y, which, final_norm=False, i=i, gains=gains):
            return _ffn(y, gains[2 * which],
                        ffn_w_gate[i, which].astype(BF16),
                        ffn_w_up[i, which].astype(BF16),
                        ffn_w_down[i, which].astype(BF16),
                        fg, final_norm=final_norm)

        y = half_ffn(y, 0)
        m = i // 2
        if i % 2 == 0:
            w_qkv = attn_w_qkv[m]
            q, k, vt, kmean = _qkv(y, gains[1], w_qkv[:, :2 * d].astype(BF16),
                                   w_qkv[:, 2 * d:].T.astype(BF16))
            o = _attention(slopes, q, k, vt, kmean.reshape(-1, d))
            y = _oproj(y, o, attn_w_o[m].astype(BF16))
        else:
            y = _pool(y, gains[1], pool_w[m].astype(BF16),
                      pool_scale[m].reshape(1, d))
        y = half_ffn(y, 1, final_norm=(i == depth - 1))
    return y.reshape(b, s, d)
```

```python
import functools

import jax
import jax.numpy as jnp
from jax import lax
from jax.experimental import pallas as pl
from jax.experimental.pallas import tpu as pltpu

N_HEADS = 16
HEAD_DIM = 64
MOBA_BLOCK = 256
MOBA_TOPK = 3
POOL_WINDOWS = (2, 4, 8, 16)
EPS = 1e-6
NEG = -1e30
LOG2E = 1.4426950408889634
UNDERFLOW = 140.0
NORM_SLACK = 1.02
NORM_CHUNK = 2048
PAIRS_PER_ITER = 8

LANES = 128
HEADS_PER_STEP = LANES // HEAD_DIM
ONES_ROWS = 16
POOL_HALO = 16
VMEM_LIMIT = 56 * 1024 * 1024

F32 = jnp.float32
BF16 = jnp.bfloat16


def _params():
    return pltpu.CompilerParams(
        dimension_semantics=("arbitrary",), vmem_limit_bytes=VMEM_LIMIT)


def _resident(shape):
    return pl.BlockSpec(shape, lambda *_: (0,) * len(shape),
                        pipeline_mode=pl.Buffered(1))


def _rmsnorm(x, g):
    ms = jnp.mean(x * x, axis=-1, keepdims=True)
    return x * lax.rsqrt(ms + EPS) * g


def _ffn_kernel(x_ref, g_ref, wg_ref, wu_ref, wd_ref, fg_ref, *rest,
                ff_chunk, final_norm, mixer_proj):
    x = x_ref[...]
    if mixer_proj:
        attn_ref, wo_ref, o_ref, a_ref = rest
        x = x + jnp.dot(attn_ref[...], wo_ref[...], preferred_element_type=F32)
    else:
        o_ref, a_ref = rest
    h = _rmsnorm(x, g_ref[...]).astype(BF16)
    d_ff = wg_ref.shape[1]
    for c in range(d_ff // ff_chunk):
        sl = slice(c * ff_chunk, (c + 1) * ff_chunk)
        gate = jnp.dot(h, wg_ref[:, sl], preferred_element_type=F32)
        up = jnp.dot(h, wu_ref[:, sl], preferred_element_type=F32)
        a_ref[:, sl] = (gate * jax.nn.sigmoid(gate) * up).astype(BF16)
    y = x + 0.5 * jnp.dot(a_ref[...], wd_ref[...], preferred_element_type=F32)
    if final_norm:
        y = _rmsnorm(y, fg_ref[...])
    o_ref[...] = y


def _ffn(x, gain, wg, wu, wd, final_gain, *, final_norm, attn=None, tm=512, ff_chunk=256):
    s, d = x.shape
    d_ff = wg.shape[1]
    row = pl.BlockSpec((tm, d), lambda i: (i, 0))
    in_specs = [row, _resident((1, d)), _resident((d, d_ff)),
                _resident((d, d_ff)), _resident((d_ff, d)), _resident((1, d))]
    operands = [x, gain, wg, wu, wd, final_gain]
    if attn is not None:
        in_specs += [row, _resident((d, d))]
        operands += list(attn)
    return pl.pallas_call(
        functools.partial(_ffn_kernel, ff_chunk=ff_chunk, final_norm=final_norm,
                          mixer_proj=attn is not None),
        grid=(s // tm,),
        in_specs=in_specs,
        out_specs=row,
        out_shape=jax.ShapeDtypeStruct((s, d), F32),
        scratch_shapes=[pltpu.VMEM((tm, d_ff), BF16)],
        compiler_params=_params(),
        name="ffn",
    )(*operands)


def _qkv_kernel(x_ref, g_ref, wqk_ref, wvt_ref, q_ref, k_ref, vt_ref, km_ref):
    d = x_ref.shape[1]
    h = _rmsnorm(x_ref[...], g_ref[...]).astype(BF16)
    qk = jnp.dot(h, wqk_ref[...], preferred_element_type=F32)
    q_ref[...] = (qk[:, :d] * (HEAD_DIM ** -0.5 * LOG2E)).astype(BF16)
    k = qk[:, d:]
    k_ref[...] = k.astype(BF16)
    vt = lax.dot_general(wvt_ref[...], h, (((1,), (1,)), ((), ())),
                         preferred_element_type=F32).astype(BF16)
    for b in range(x_ref.shape[0] // MOBA_BLOCK):
        rows = slice(b * MOBA_BLOCK, (b + 1) * MOBA_BLOCK)
        km_ref[b] = jnp.mean(k[rows], axis=0, keepdims=True)
        vt_ref[b] = vt[:, rows]


def _qkv(x, gain, wqk, wvt, *, tm=512):
    s, d = x.shape
    row = pl.BlockSpec((tm, d), lambda i: (i, 0))
    nb_tile = tm // MOBA_BLOCK
    nb = s // MOBA_BLOCK
    return pl.pallas_call(
        _qkv_kernel,
        grid=(s // tm,),
        in_specs=[row, _resident((1, d)), _resident((d, 2 * d)), _resident((d, d))],
        out_specs=[row, row,
                   pl.BlockSpec((nb_tile, d, MOBA_BLOCK), lambda i: (i, 0, 0)),
                   pl.BlockSpec((nb_tile, 1, d), lambda i: (i, 0, 0))],
        out_shape=[jax.ShapeDtypeStruct((s, d), BF16)] * 2
        + [jax.ShapeDtypeStruct((nb, d, MOBA_BLOCK), BF16),
           jax.ShapeDtypeStruct((nb, 1, d), F32)],
        compiler_params=_params(),
        name="qkv",
    )(x, gain, wqk, wvt)


def _attn_kernel(slopes_ref, q_ref, k_ref, vt_ref, km_ref, o_ref,
                 bias_ref, add_ref, kn_ref, s0_ref, s1_ref, p0_ref, p1_ref):
    pair = pl.program_id(0)
    jq = pl.program_id(1)
    bs = MOBA_BLOCK
    nb = km_ref.shape[0]
    heads = range(HEADS_PER_STEP)
    halves = [slice(h * LANES, (h + 1) * LANES) for h in range(bs // LANES)]
    s_refs = (s0_ref, s1_ref)
    p_refs = (p0_ref, p1_ref)
    q = q_ref[...]
    lane = lax.broadcasted_iota(jnp.int32, (1, LANES), 1)
    rel = (lax.broadcasted_iota(jnp.int32, (bs, bs), 1)
           - lax.broadcasted_iota(jnp.int32, (bs, bs), 0))
    blk = lax.broadcasted_iota(jnp.int32, (nb, bs), 0)
    blkf = blk.astype(F32)
    past = blk < jq

    lane_head = (lax.broadcasted_iota(jnp.int32, (LANES, LANES), 0) // HEAD_DIM
                 == lax.broadcasted_iota(jnp.int32, (LANES, LANES), 1)).astype(BF16)

    def max_head_norm_sq(x):
        xf = x.astype(F32)
        sq = jnp.dot((xf * xf).astype(BF16), lane_head, preferred_element_type=F32)
        return jnp.max(sq, axis=0, keepdims=True)

    @pl.when(jq == 0)
    def _():
        for hh in heads:
            bias_ref[hh] = slopes_ref[pair * HEADS_PER_STEP + hh] * rel.astype(F32)
        kn = jnp.zeros((1, LANES), F32)
        for c in range(k_ref.shape[0] // NORM_CHUNK):
            kn = jnp.maximum(kn, max_head_norm_sq(k_ref[c * NORM_CHUNK:(c + 1) * NORM_CHUNK, :]))
        kn_ref[...] = kn

    qhs = []
    for hh in heads:
        in_head = (lane >= hh * HEAD_DIM) & (lane < (hh + 1) * HEAD_DIM)
        qh = jnp.where(in_head, q, jnp.zeros_like(q))
        qhs.append(qh.astype(F32).T.astype(BF16))

    def block_scores(n, hh):
        keys = k_ref[pl.ds(pl.multiple_of(n * bs, bs), bs), :]
        return jnp.dot(keys, qhs[hh], preferred_element_type=F32)

    ones = jnp.ones((ONES_ROWS, bs), BF16)

    def weighted_values(n, slot, hh):
        lhs = jnp.concatenate([vt_ref[n][hh * HEAD_DIM:(hh + 1) * HEAD_DIM, :], ones], axis=0)
        return jnp.dot(lhs, p_refs[slot][hh], preferred_element_type=F32)

    i_stop = (jq + 1) // 2
    n_top = 2 * i_stop - 1
    s_own = [block_scores(jq, hh) for hh in heads]
    for hh in heads:
        s_refs[0][hh] = block_scores(jnp.maximum(n_top, 0), hh)

    km = km_ref[...]
    km_hi = km.astype(BF16)
    rest = km - km_hi.astype(F32)
    km_mid = rest.astype(BF16)
    km_lo = (rest - km_mid.astype(F32)).astype(BF16)
    km_terms = jnp.concatenate([km_hi, km_mid, km_lo], axis=0)
    for hh in heads:
        g3 = jnp.dot(km_terms, qhs[hh], preferred_element_type=F32)
        gate = g3[:nb] + g3[nb:2 * nb] + g3[2 * nb:]
        gate = jnp.where(past, gate, NEG)
        sel = jnp.zeros((nb, bs), jnp.bool_)
        for _ in range(MOBA_TOPK):
            top = jnp.max(gate, axis=0, keepdims=True)
            first = jnp.min(jnp.where(gate == top, blkf, float(nb)), axis=0, keepdims=True)
            pick = blkf == first
            sel = sel | (pick & past)
            gate = jnp.where(pick, -jnp.inf, gate)
        block_bias = (-slopes_ref[pair * HEADS_PER_STEP + hh] * bs) * (jq - blk).astype(F32)
        add_ref[hh] = jnp.where(sel, block_bias, NEG)

    causal = jnp.where(rel >= 0, 0.0, NEG)
    init = []
    for hh in heads:
        m = []
        for cols in halves:
            sb = s_own[hh][:, cols] - bias_ref[hh, :, cols] + causal[:, cols]
            m.append(jnp.max(sb, axis=0, keepdims=True))
            p_refs[1][hh, :, cols] = jnp.exp2(sb - m[-1]).astype(BF16)
        init.append((tuple(m), jnp.zeros((1, bs), F32), jnp.zeros((HEAD_DIM, bs), F32),
                     jnp.ones((1, bs), F32)))

    qk_bound = jnp.sqrt(max_head_norm_sq(q) * kn_ref[...] * NORM_SLACK)
    first_block = None
    for hh in heads:
        m_min = jnp.min(jnp.minimum(*init[hh][0]), axis=1, keepdims=True)
        reach = (UNDERFLOW + qk_bound[:, hh:hh + 1] - m_min) / slopes_ref[pair * HEADS_PER_STEP + hh]
        near = jq.astype(F32) - jnp.floor((reach - 1.0) / bs) - 1.0
        near = jnp.where(near > 0.0, near, 0.0)
        first_block = near if first_block is None else jnp.minimum(first_block, near)
    i_start = first_block[0, 0].astype(jnp.int32) // 2

    def step(n, n_prev, cur, carry):
        folded = []
        for hh in heads:
            _, l, acc, alpha_prev = carry[hh]
            r = weighted_values(n_prev, 1 - cur, hh)
            folded.append((alpha_prev * l + r[HEAD_DIM:HEAD_DIM + 1],
                           alpha_prev * acc + r[:HEAD_DIM]))
        n_next = jnp.maximum(n - 1, 0)
        for hh in heads:
            s_refs[1 - cur][hh] = block_scores(n_next, hh)
        out = []
        for hh in heads:
            m = carry[hh][0]
            row = add_ref[hh, pl.ds(n, 1), :]
            m_new, alpha = [], []
            for h, cols in enumerate(halves):
                sb = s_refs[cur][hh, :, cols] - bias_ref[hh, :, cols]
                top = jnp.max(sb, axis=0, keepdims=True) + row[:, cols]
                m_new.append(jnp.maximum(m[h], top))
                p_refs[cur][hh, :, cols] = jnp.exp2(sb - (m_new[h] - row[:, cols])).astype(BF16)
                alpha.append(jnp.exp2(m[h] - m_new[h]))
            out.append((tuple(m_new), *folded[hh], jnp.concatenate(alpha, axis=1)))
        return tuple(out)

    def pairs(t0, count, carry):
        for t in range(count):
            n = n_top - 2 * (t0 + t)
            carry = step(n, jnp.where(n == n_top, jq, n + 1), 0, carry)
            carry = step(n - 1, n, 1, carry)
        return carry

    n_pairs = i_stop - i_start
    n_long = n_pairs // PAIRS_PER_ITER
    fin = lax.fori_loop(0, n_long,
                        lambda j, c: pairs(PAIRS_PER_ITER * j, PAIRS_PER_ITER, c), tuple(init))
    done = PAIRS_PER_ITER * n_long
    size = PAIRS_PER_ITER // 2
    while size:
        take = ((n_pairs - done) // size) % 2
        fin = lax.fori_loop(0, take, lambda _, c, done=done, size=size: pairs(done, size, c), fin)
        done = done + take * size
        size //= 2
    n_last = jnp.where(n_pairs == 0, jq, 2 * i_start)
    o_t = []
    for hh in heads:
        _, l, acc, alpha = fin[hh]
        r = weighted_values(n_last, 1, hh)
        o_t.append((alpha * acc + r[:HEAD_DIM]) / (alpha * l + r[HEAD_DIM:HEAD_DIM + 1]))
    o_ref[...] = jnp.concatenate(o_t, axis=0).T.astype(BF16)


def _attention(slopes, q, k, vt, kmean):
    s, d = q.shape
    nb = kmean.shape[0]
    tile = pl.BlockSpec((MOBA_BLOCK, LANES), lambda p, j: (j, p))
    return pl.pallas_call(
        _attn_kernel,
        grid=(d // LANES, s // MOBA_BLOCK),
        in_specs=[pl.BlockSpec(memory_space=pltpu.SMEM), tile,
                  pl.BlockSpec((s, LANES), lambda p, j: (0, p)),
                  pl.BlockSpec((nb, LANES, MOBA_BLOCK), lambda p, j: (0, p, 0)),
                  pl.BlockSpec((nb, LANES), lambda p, j: (0, p))],
        out_specs=tile,
        out_shape=jax.ShapeDtypeStruct((s, d), BF16),
        scratch_shapes=[pltpu.VMEM((HEADS_PER_STEP, MOBA_BLOCK, MOBA_BLOCK), F32),
                        pltpu.VMEM((HEADS_PER_STEP, nb, MOBA_BLOCK), F32),
                        pltpu.VMEM((1, LANES), F32),
                        pltpu.VMEM((HEADS_PER_STEP, MOBA_BLOCK, MOBA_BLOCK), F32),
                        pltpu.VMEM((HEADS_PER_STEP, MOBA_BLOCK, MOBA_BLOCK), F32),
                        pltpu.VMEM((HEADS_PER_STEP, MOBA_BLOCK, MOBA_BLOCK), BF16),
                        pltpu.VMEM((HEADS_PER_STEP, MOBA_BLOCK, MOBA_BLOCK), BF16)],
        compiler_params=pltpu.CompilerParams(
            dimension_semantics=("arbitrary", "arbitrary"),
            vmem_limit_bytes=VMEM_LIMIT),
        name="moba_attention",
    )(slopes, q, k, vt, kmean)


def _pool_kernel(x_ref, g_ref, w_ref, sc_ref, o_ref, h_ref):
    i = pl.program_id(0)
    tm, d = x_ref.shape
    gd = d // len(POOL_WINDOWS)
    x = x_ref[...]

    @pl.when(i == 0)
    def _():
        h_ref[0:POOL_HALO, :] = jnp.zeros((POOL_HALO, d), F32)

    h_ref[POOL_HALO:, :] = _rmsnorm(x, g_ref[...])
    t = i * tm + lax.broadcasted_iota(jnp.int32, (tm, 1), 0)
    for g, w in enumerate(POOL_WINDOWS):
        cols = slice(g * gd, (g + 1) * gd)
        win = h_ref[POOL_HALO:, cols]
        for back in range(1, w):
            win = win + h_ref[POOL_HALO - back:POOL_HALO - back + tm, cols]
        cnt = jnp.minimum(t + 1, w).astype(F32)
        y = (win / cnt - h_ref[POOL_HALO:, cols]).astype(BF16)
        mixed = jnp.dot(y, w_ref[g], preferred_element_type=F32)
        o_ref[:, cols] = x[:, cols] + mixed * sc_ref[:, cols]
    h_ref[0:POOL_HALO, :] = h_ref[tm:tm + POOL_HALO, :]


def _pool(x, gain, w, scale, *, tm=512):
    s, d = x.shape
    row = pl.BlockSpec((tm, d), lambda i: (i, 0))
    return pl.pallas_call(
        _pool_kernel,
        grid=(s // tm,),
        in_specs=[row, _resident((1, d)), _resident(w.shape), _resident((1, d))],
        out_specs=row,
        out_shape=jax.ShapeDtypeStruct((s, d), F32),
        scratch_shapes=[pltpu.VMEM((tm + POOL_HALO, d), F32)],
        compiler_params=_params(),
        name="pool_mixer",
    )(x, gain, w, scale)


def kernel(x, ln_gains, ffn_w_gate, ffn_w_up, ffn_w_down, attn_w_qkv, attn_w_o,
           pool_w, pool_scale, final_gain):
    b, s, d = x.shape
    assert b == 1
    depth = ln_gains.shape[0]
    slopes = LOG2E * jnp.exp2(-8.0 * jnp.arange(1, N_HEADS + 1, dtype=F32) / N_HEADS)
    fg = final_gain.reshape(1, d)
    y = x.reshape(s, d)
    for i in range(depth):
        gains = ln_gains[i].reshape(3, 1, d)

        def half_ffn(y, which, final_norm=False, attn=None, i=i, gains=gains):
            return _ffn(y, gains[2 * which],
                        ffn_w_gate[i, which].astype(BF16),
                        ffn_w_up[i, which].astype(BF16),
                        ffn_w_down[i, which].astype(BF16),
                        fg, final_norm=final_norm, attn=attn)

        y = half_ffn(y, 0)
        m = i // 2
        attn = None
        if i % 2 == 0:
            w_qkv = attn_w_qkv[m]
            q, k, vt, kmean = _qkv(y, gains[1], w_qkv[:, :2 * d].astype(BF16),
                                   w_qkv[:, 2 * d:].T.astype(BF16))
            attn = (_attention(slopes, q, k, vt, kmean.reshape(-1, d)),
                    attn_w_o[m].astype(BF16))
        else:
            y = _pool(y, gains[1], pool_w[m].astype(BF16),
                      pool_scale[m].reshape(1, d))
        y = half_ffn(y, 1, final_norm=(i == depth - 1), attn=attn)
    return y.reshape(b, s, d)
```

```python
import functools

import jax
import jax.numpy as jnp
from jax import lax
from jax.experimental import pallas as pl
from jax.experimental.pallas import tpu as pltpu

N_HEADS = 16
HEAD_DIM = 64
MOBA_BLOCK = 256
MOBA_TOPK = 3
POOL_WINDOWS = (2, 4, 8, 16)
EPS = 1e-6
NEG = -1e30
LOG2E = 1.4426950408889634
UNDERFLOW = 140.0
NORM_SLACK = 1.02
NORM_CHUNK = 2048
PAIRS_PER_ITER = 8

LANES = 128
HEADS_PER_STEP = LANES // HEAD_DIM
ONES_ROWS = 16
POOL_LEVEL_STEP = 8
POOL_HALO = 32
VMEM_LIMIT = 56 * 1024 * 1024

F32 = jnp.float32
BF16 = jnp.bfloat16


def _params():
    return pltpu.CompilerParams(
        dimension_semantics=("arbitrary",), vmem_limit_bytes=VMEM_LIMIT)


def _resident(shape):
    return pl.BlockSpec(shape, lambda *_: (0,) * len(shape),
                        pipeline_mode=pl.Buffered(1))


def _rmsnorm(x, g):
    ms = jnp.mean(x * x, axis=-1, keepdims=True)
    return x * lax.rsqrt(ms + EPS) * g


def _ffn_kernel(x_ref, g_ref, wg_ref, wu_ref, wd_ref, fg_ref, *rest,
                ff_chunk, final_norm, mixer_proj):
    x = x_ref[...]
    if mixer_proj:
        attn_ref, wo_ref, o_ref, a_ref = rest
        x = x + jnp.dot(attn_ref[...], wo_ref[...], preferred_element_type=F32)
    else:
        o_ref, a_ref = rest
    h = _rmsnorm(x, g_ref[...]).astype(BF16)
    d_ff = wg_ref.shape[1]
    for c in range(d_ff // ff_chunk):
        sl = slice(c * ff_chunk, (c + 1) * ff_chunk)
        gate = jnp.dot(h, wg_ref[:, sl], preferred_element_type=F32)
        up = jnp.dot(h, wu_ref[:, sl], preferred_element_type=F32)
        a_ref[:, sl] = (gate * jax.nn.sigmoid(gate) * up).astype(BF16)
    y = x + 0.5 * jnp.dot(a_ref[...], wd_ref[...], preferred_element_type=F32)
    if final_norm:
        y = _rmsnorm(y, fg_ref[...])
    o_ref[...] = y


def _ffn(x, gain, wg, wu, wd, final_gain, *, final_norm, attn=None, tm=512, ff_chunk=256):
    s, d = x.shape
    d_ff = wg.shape[1]
    row = pl.BlockSpec((tm, d), lambda i: (i, 0))
    in_specs = [row, _resident((1, d)), _resident((d, d_ff)),
                _resident((d, d_ff)), _resident((d_ff, d)), _resident((1, d))]
    operands = [x, gain, wg, wu, wd, final_gain]
    if attn is not None:
        in_specs += [row, _resident((d, d))]
        operands += list(attn)
    return pl.pallas_call(
        functools.partial(_ffn_kernel, ff_chunk=ff_chunk, final_norm=final_norm,
                          mixer_proj=attn is not None),
        grid=(s // tm,),
        in_specs=in_specs,
        out_specs=row,
        out_shape=jax.ShapeDtypeStruct((s, d), F32),
        scratch_shapes=[pltpu.VMEM((tm, d_ff), BF16)],
        compiler_params=_params(),
        name="ffn",
    )(*operands)


def _qkv_kernel(x_ref, g_ref, wqk_ref, wvt_ref, q_ref, k_ref, vt_ref, km_ref):
    d = x_ref.shape[1]
    h = _rmsnorm(x_ref[...], g_ref[...]).astype(BF16)
    qk = jnp.dot(h, wqk_ref[...], preferred_element_type=F32)
    q_ref[...] = (qk[:, :d] * (HEAD_DIM ** -0.5 * LOG2E)).astype(BF16)
    k = qk[:, d:]
    k_ref[...] = k.astype(BF16)
    vt = lax.dot_general(wvt_ref[...], h, (((1,), (1,)), ((), ())),
                         preferred_element_type=F32).astype(BF16)
    for b in range(x_ref.shape[0] // MOBA_BLOCK):
        rows = slice(b * MOBA_BLOCK, (b + 1) * MOBA_BLOCK)
        km_ref[b] = jnp.mean(k[rows], axis=0, keepdims=True)
        vt_ref[b] = vt[:, rows]


def _qkv(x, gain, wqk, wvt, *, tm=512):
    s, d = x.shape
    row = pl.BlockSpec((tm, d), lambda i: (i, 0))
    nb_tile = tm // MOBA_BLOCK
    nb = s // MOBA_BLOCK
    return pl.pallas_call(
        _qkv_kernel,
        grid=(s // tm,),
        in_specs=[row, _resident((1, d)), _resident((d, 2 * d)), _resident((d, d))],
        out_specs=[row, row,
                   pl.BlockSpec((nb_tile, d, MOBA_BLOCK), lambda i: (i, 0, 0)),
                   pl.BlockSpec((nb_tile, 1, d), lambda i: (i, 0, 0))],
        out_shape=[jax.ShapeDtypeStruct((s, d), BF16)] * 2
        + [jax.ShapeDtypeStruct((nb, d, MOBA_BLOCK), BF16),
           jax.ShapeDtypeStruct((nb, 1, d), F32)],
        compiler_params=_params(),
        name="qkv",
    )(x, gain, wqk, wvt)


def _attn_kernel(slopes_ref, q_ref, k_ref, vt_ref, km_ref, o_ref,
                 bias_ref, add_ref, kn_ref, s0_ref, s1_ref, p0_ref, p1_ref):
    pair = pl.program_id(0)
    jq = pl.program_id(1)
    bs = MOBA_BLOCK
    nb = km_ref.shape[0]
    heads = range(HEADS_PER_STEP)
    halves = [slice(h * LANES, (h + 1) * LANES) for h in range(bs // LANES)]
    s_refs = (s0_ref, s1_ref)
    p_refs = (p0_ref, p1_ref)
    q = q_ref[...]
    lane = lax.broadcasted_iota(jnp.int32, (1, LANES), 1)
    rel = (lax.broadcasted_iota(jnp.int32, (bs, bs), 1)
           - lax.broadcasted_iota(jnp.int32, (bs, bs), 0))
    blk = lax.broadcasted_iota(jnp.int32, (nb, bs), 0)
    blkf = blk.astype(F32)
    past = blk < jq

    lane_head = (lax.broadcasted_iota(jnp.int32, (LANES, LANES), 0) // HEAD_DIM
                 == lax.broadcasted_iota(jnp.int32, (LANES, LANES), 1)).astype(BF16)

    def max_head_norm_sq(x):
        xf = x.astype(F32)
        sq = jnp.dot((xf * xf).astype(BF16), lane_head, preferred_element_type=F32)
        return jnp.max(sq, axis=0, keepdims=True)

    @pl.when(jq == 0)
    def _():
        for hh in heads:
            bias_ref[hh] = slopes_ref[pair * HEADS_PER_STEP + hh] * rel.astype(F32)
        kn = jnp.zeros((1, LANES), F32)
        for c in range(k_ref.shape[0] // NORM_CHUNK):
            kn = jnp.maximum(kn, max_head_norm_sq(k_ref[c * NORM_CHUNK:(c + 1) * NORM_CHUNK, :]))
        kn_ref[...] = kn

    qhs = []
    for hh in heads:
        in_head = (lane >= hh * HEAD_DIM) & (lane < (hh + 1) * HEAD_DIM)
        qh = jnp.where(in_head, q, jnp.zeros_like(q))
        qhs.append(qh.astype(F32).T.astype(BF16))

    def block_scores(n, hh):
        keys = k_ref[pl.ds(pl.multiple_of(n * bs, bs), bs), :]
        return jnp.dot(keys, qhs[hh], preferred_element_type=F32)

    ones = jnp.ones((ONES_ROWS, bs), BF16)

    def weighted_values(n, slot, hh):
        lhs = jnp.concatenate([vt_ref[n][hh * HEAD_DIM:(hh + 1) * HEAD_DIM, :], ones], axis=0)
        return jnp.dot(lhs, p_refs[slot][hh], preferred_element_type=F32)

    i_stop = (jq + 1) // 2
    n_top = 2 * i_stop - 1
    s_own = [block_scores(jq, hh) for hh in heads]
    for hh in heads:
        s_refs[0][hh] = block_scores(jnp.maximum(n_top, 0), hh)

    km = km_ref[...]
    km_hi = km.astype(BF16)
    rest = km - km_hi.astype(F32)
    km_mid = rest.astype(BF16)
    km_lo = (rest - km_mid.astype(F32)).astype(BF16)
    km_terms = jnp.concatenate([km_hi, km_mid, km_lo], axis=0)
    for hh in heads:
        g3 = jnp.dot(km_terms, qhs[hh], preferred_element_type=F32)
        gate = g3[:nb] + g3[nb:2 * nb] + g3[2 * nb:]
        gate = jnp.where(past, gate, NEG)
        sel = jnp.zeros((nb, bs), jnp.bool_)
        for _ in range(MOBA_TOPK):
            top = jnp.max(gate, axis=0, keepdims=True)
            first = jnp.min(jnp.where(gate == top, blkf, float(nb)), axis=0, keepdims=True)
            pick = blkf == first
            sel = sel | (pick & past)
            gate = jnp.where(pick, -jnp.inf, gate)
        block_bias = (-slopes_ref[pair * HEADS_PER_STEP + hh] * bs) * (jq - blk).astype(F32)
        add_ref[hh] = jnp.where(sel, block_bias, NEG)

    causal = jnp.where(rel >= 0, 0.0, NEG)
    init = []
    for hh in heads:
        m = []
        for cols in halves:
            sb = s_own[hh][:, cols] - bias_ref[hh, :, cols] + causal[:, cols]
            m.append(jnp.max(sb, axis=0, keepdims=True))
            p_refs[1][hh, :, cols] = jnp.exp2(sb - m[-1]).astype(BF16)
        init.append((tuple(m), jnp.zeros((1, bs), F32), jnp.zeros((HEAD_DIM, bs), F32),
                     jnp.ones((1, bs), F32)))

    qk_bound = jnp.sqrt(max_head_norm_sq(q) * kn_ref[...] * NORM_SLACK)
    first_block = None
    for hh in heads:
        m_min = jnp.min(jnp.minimum(*init[hh][0]), axis=1, keepdims=True)
        reach = (UNDERFLOW + qk_bound[:, hh:hh + 1] - m_min) / slopes_ref[pair * HEADS_PER_STEP + hh]
        near = jq.astype(F32) - jnp.floor((reach - 1.0) / bs) - 1.0
        near = jnp.where(near > 0.0, near, 0.0)
        first_block = near if first_block is None else jnp.minimum(first_block, near)
    i_start = first_block[0, 0].astype(jnp.int32) // 2

    def step(n, n_prev, cur, carry):
        folded = []
        for hh in heads:
            _, l, acc, alpha_prev = carry[hh]
            r = weighted_values(n_prev, 1 - cur, hh)
            folded.append((alpha_prev * l + r[HEAD_DIM:HEAD_DIM + 1],
                           alpha_prev * acc + r[:HEAD_DIM]))
        n_next = jnp.maximum(n - 1, 0)
        for hh in heads:
            s_refs[1 - cur][hh] = block_scores(n_next, hh)
        out = []
        for hh in heads:
            m = carry[hh][0]
            row = add_ref[hh, pl.ds(n, 1), :]
            m_new, alpha = [], []
            for h, cols in enumerate(halves):
                sb = s_refs[cur][hh, :, cols] - bias_ref[hh, :, cols]
                top = jnp.max(sb, axis=0, keepdims=True) + row[:, cols]
                m_new.append(jnp.maximum(m[h], top))
                p_refs[cur][hh, :, cols] = jnp.exp2(sb - (m_new[h] - row[:, cols])).astype(BF16)
                alpha.append(jnp.exp2(m[h] - m_new[h]))
            out.append((tuple(m_new), *folded[hh], jnp.concatenate(alpha, axis=1)))
        return tuple(out)

    def pairs(t0, count, carry):
        for t in range(count):
            n = n_top - 2 * (t0 + t)
            carry = step(n, jnp.where(n == n_top, jq, n + 1), 0, carry)
            carry = step(n - 1, n, 1, carry)
        return carry

    n_pairs = i_stop - i_start
    n_long = n_pairs // PAIRS_PER_ITER
    fin = lax.fori_loop(0, n_long,
                        lambda j, c: pairs(PAIRS_PER_ITER * j, PAIRS_PER_ITER, c), tuple(init))
    done = PAIRS_PER_ITER * n_long
    size = PAIRS_PER_ITER // 2
    while size:
        take = ((n_pairs - done) // size) % 2
        fin = lax.fori_loop(0, take, lambda _, c, done=done, size=size: pairs(done, size, c), fin)
        done = done + take * size
        size //= 2
    n_last = jnp.where(n_pairs == 0, jq, 2 * i_start)
    o_t = []
    for hh in heads:
        _, l, acc, alpha = fin[hh]
        r = weighted_values(n_last, 1, hh)
        o_t.append((alpha * acc + r[:HEAD_DIM]) / (alpha * l + r[HEAD_DIM:HEAD_DIM + 1]))
    o_ref[...] = jnp.concatenate(o_t, axis=0).T.astype(BF16)


def _attention(slopes, q, k, vt, kmean):
    s, d = q.shape
    nb = kmean.shape[0]
    tile = pl.BlockSpec((MOBA_BLOCK, LANES), lambda p, j: (j, p))
    return pl.pallas_call(
        _attn_kernel,
        grid=(d // LANES, s // MOBA_BLOCK),
        in_specs=[pl.BlockSpec(memory_space=pltpu.SMEM), tile,
                  pl.BlockSpec((s, LANES), lambda p, j: (0, p)),
                  pl.BlockSpec((nb, LANES, MOBA_BLOCK), lambda p, j: (0, p, 0)),
                  pl.BlockSpec((nb, LANES), lambda p, j: (0, p))],
        out_specs=tile,
        out_shape=jax.ShapeDtypeStruct((s, d), BF16),
        scratch_shapes=[pltpu.VMEM((HEADS_PER_STEP, MOBA_BLOCK, MOBA_BLOCK), F32),
                        pltpu.VMEM((HEADS_PER_STEP, nb, MOBA_BLOCK), F32),
                        pltpu.VMEM((1, LANES), F32),
                        pltpu.VMEM((HEADS_PER_STEP, MOBA_BLOCK, MOBA_BLOCK), F32),
                        pltpu.VMEM((HEADS_PER_STEP, MOBA_BLOCK, MOBA_BLOCK), F32),
                        pltpu.VMEM((HEADS_PER_STEP, MOBA_BLOCK, MOBA_BLOCK), BF16),
                        pltpu.VMEM((HEADS_PER_STEP, MOBA_BLOCK, MOBA_BLOCK), BF16)],
        compiler_params=pltpu.CompilerParams(
            dimension_semantics=("arbitrary", "arbitrary"),
            vmem_limit_bytes=VMEM_LIMIT),
        name="moba_attention",
    )(slopes, q, k, vt, kmean)


def _pool_kernel(x_ref, g_ref, w_ref, sc_ref, o_ref, h_ref, lvl0_ref, lvl1_ref):
    i = pl.program_id(0)
    tm, d = x_ref.shape
    gd = d // len(POOL_WINDOWS)
    x = x_ref[...]
    lvl_refs = (lvl0_ref, lvl1_ref)
    end = tm + POOL_HALO

    @pl.when(i == 0)
    def _():
        h_ref[0:POOL_HALO, :] = jnp.zeros((POOL_HALO, d), F32)

    h_ref[POOL_HALO:, :] = _rmsnorm(x, g_ref[...])
    t = i * tm + lax.broadcasted_iota(jnp.int32, (tm, 1), 0)
    for g, w in enumerate(POOL_WINDOWS):
        cols = slice(g * gd, (g + 1) * gd)
        n_levels = w.bit_length() - 1
        src = None
        for j in range(1, n_levels + 1):
            shift = 2 ** (j - 1)
            start = POOL_HALO if j == n_levels else POOL_LEVEL_STEP * j
            if src is None:
                val = h_ref[start:, cols] + h_ref[start - shift:end - shift, cols]
            else:
                val = src[start:, :] + src[start - shift:end - shift, :]
            if j < n_levels:
                src = lvl_refs[j % 2]
                src[start:, :] = val
        win = val
        cnt = jnp.minimum(t + 1, w).astype(F32)
        y = (win / cnt - h_ref[POOL_HALO:, cols]).astype(BF16)
        mixed = jnp.dot(y, w_ref[g], preferred_element_type=F32)
        o_ref[:, cols] = x[:, cols] + mixed * sc_ref[:, cols]
    h_ref[0:POOL_HALO, :] = h_ref[tm:tm + POOL_HALO, :]


def _pool(x, gain, w, scale, *, tm=512):
    s, d = x.shape
    row = pl.BlockSpec((tm, d), lambda i: (i, 0))
    return pl.pallas_call(
        _pool_kernel,
        grid=(s // tm,),
        in_specs=[row, _resident((1, d)), _resident(w.shape), _resident((1, d))],
        out_specs=row,
        out_shape=jax.ShapeDtypeStruct((s, d), F32),
        scratch_shapes=[pltpu.VMEM((tm + POOL_HALO, d), F32)]
        + [pltpu.VMEM((tm + POOL_HALO, d // len(POOL_WINDOWS)), F32)] * 2,
        compiler_params=_params(),
        name="pool_mixer",
    )(x, gain, w, scale)


def kernel(x, ln_gains, ffn_w_gate, ffn_w_up, ffn_w_down, attn_w_qkv, attn_w_o,
           pool_w, pool_scale, final_gain):
    b, s, d = x.shape
    assert b == 1
    depth = ln_gains.shape[0]
    slopes = LOG2E * jnp.exp2(-8.0 * jnp.arange(1, N_HEADS + 1, dtype=F32) / N_HEADS)
    fg = final_gain.reshape(1, d)
    y = x.reshape(s, d)
    for i in range(depth):
        gains = ln_gains[i].reshape(3, 1, d)

        def half_ffn(y, which, final_norm=False, attn=None, i=i, gains=gains):
            return _ffn(y, gains[2 * which],
                        ffn_w_gate[i, which].astype(BF16),
                        ffn_w_up[i, which].astype(BF16),
                        ffn_w_down[i, which].astype(BF16),
                        fg, final_norm=final_norm, attn=attn)

        y = half_ffn(y, 0)
        m = i // 2
        attn = None
        if i % 2 == 0:
            w_qkv = attn_w_qkv[m]
            q, k, vt, kmean = _qkv(y, gains[1], w_qkv[:, :2 * d].astype(BF16),
                                   w_qkv[:, 2 * d:].T.astype(BF16))
            attn = (_attention(slopes, q, k, vt, kmean.reshape(-1, d)),
                    attn_w_o[m].astype(BF16))
        else:
            y = _pool(y, gains[1], pool_w[m].astype(BF16),
                      pool_scale[m].reshape(1, d))
        y = half_ffn(y, 1, final_norm=(i == depth - 1), attn=attn)
    return y.reshape(b, s, d)
```
